```python
import math
import jax, jax.numpy as jnp
from jax import lax
import numpy as np

D_MODEL = 4096
BATCH = 4
SEQ = 2048
DEPTH = 2
DEC_BATCH = 8
DEC_SEQ = 4
PAST_LEN = 16384
PAGE_SIZE = 128

N_MIXERS = 2
N_ATTN_LAYERS = (DEPTH + 1) // 2
N_GMLP_LAYERS = DEPTH // 2
D_FF = 11008
N_HEADS = 16
HEAD_DIM = 128
V_HEAD_DIM = 2 * HEAD_DIM
ROT_DIM = HEAD_DIM // 4
ROPE_THETA = 500000.0
Q_BLOCK = 128
CHUNK = 128
GMLP_WIDTH = D_MODEL
GMLP_GROUPS = 8
GMLP_GROUP_DIM = GMLP_WIDTH // GMLP_GROUPS
EPS = 1e-6

kernel_name = 'hybrid_diffattn_gmlp_macaron_step'


def rms_norm(x, g):
    xf = x.astype(jnp.float32)
    y = xf * lax.rsqrt(jnp.mean(xf * xf, axis=-1, keepdims=True) + EPS)
    return (y * g.astype(jnp.float32)).astype(x.dtype)


def layer_norm(x, g, b):
    xf = x.astype(jnp.float32)
    mu = jnp.mean(xf, axis=-1, keepdims=True)
    xc = xf - mu
    y = xc * lax.rsqrt(jnp.mean(xc * xc, axis=-1, keepdims=True) + EPS)
    return (y * g.astype(jnp.float32) + b.astype(jnp.float32)).astype(x.dtype)


def swiglu(x, wg, wu, wd):
    return (jax.nn.silu(x @ wg) * (x @ wu)) @ wd


def rope_partial(x, pos):
    half = ROT_DIM // 2
    inv = ROPE_THETA ** (-jnp.arange(half, dtype=jnp.float32) / half)
    ang = pos.astype(jnp.float32)[:, None] * inv[None, :]
    cos = jnp.cos(ang)[:, None, None, :]
    sin = jnp.sin(ang)[:, None, None, :]
    xf = x.astype(jnp.float32)
    x1, x2, rest = xf[..., :half], xf[..., half:ROT_DIM], xf[..., ROT_DIM:]
    out = jnp.concatenate([x1 * cos - x2 * sin, x2 * cos + x1 * sin, rest], axis=-1)
    return out.astype(x.dtype)


def diff_lambda(lq1, lk1, lq2, lk2, lambda_init):
    f32 = jnp.float32
    return (jnp.exp(jnp.sum(lq1.astype(f32) * lk1.astype(f32)))
            - jnp.exp(jnp.sum(lq2.astype(f32) * lk2.astype(f32))) + lambda_init)


def diff_qkv(h, w_in, q_gain, k_gain, pos):
    B, L, _ = h.shape
    q, k, v = jnp.split(h @ w_in, 3, axis=-1)
    q = q.reshape(B, L, N_HEADS, 2, HEAD_DIM)
    k = k.reshape(B, L, N_HEADS, 2, HEAD_DIM)
    v = v.reshape(B, L, N_HEADS, V_HEAD_DIM)
    q = rope_partial(rms_norm(q, q_gain), pos)
    k = rope_partial(rms_norm(k, k_gain), pos)
    return q, k, v


def diff_attn_prompt(q, k, v, lam):
    B, S = q.shape[0], q.shape[1]
    nb = S // Q_BLOCK
    scale = HEAD_DIM ** -0.5
    qb = q.reshape(B, nb, Q_BLOCK, N_HEADS, 2, HEAD_DIM).transpose(1, 0, 2, 3, 4, 5)
    key_pos = jnp.arange(S)

    def block(args):
        qi, bi = args
        s = jnp.einsum('bqhcd,bkhcd->bhcqk', qi, k).astype(jnp.float32) * scale
        qpos = bi * Q_BLOCK + jnp.arange(Q_BLOCK)
        mask = key_pos[None, :] <= qpos[:, None]
        p = jax.nn.softmax(jnp.where(mask, s, -jnp.inf), axis=-1)
        a = p[:, :, 0] - lam * p[:, :, 1]
        return jnp.einsum('bhqk,bkhe->bqhe', a.astype(v.dtype), v)

    out = lax.map(block, (qb, jnp.arange(nb)))
    return out.transpose(1, 0, 2, 3, 4).reshape(B, S, N_HEADS, V_HEAD_DIM)


def diff_attn_sample(q, k_new, v_new, k_past, v_past, lam):
    T = q.shape[1]
    P = k_past.shape[1]
    scale = HEAD_DIM ** -0.5
    s_past = jnp.einsum('bqhcd,bkhcd->bhcqk', q, k_past).astype(jnp.float32) * scale
    s_new = jnp.einsum('bqhcd,bkhcd->bhcqk', q, k_new).astype(jnp.float32) * scale
    causal = jnp.arange(T)[None, :] <= jnp.arange(T)[:, None]
    s_new = jnp.where(causal, s_new, -jnp.inf)
    p = jax.nn.softmax(jnp.concatenate([s_past, s_new], axis=-1), axis=-1)
    a = (p[:, :, 0] - lam * p[:, :, 1]).astype(v_new.dtype)
    return (jnp.einsum('bhqk,bkhe->bqhe', a[..., :P], v_past)
            + jnp.einsum('bhqk,bkhe->bqhe', a[..., P:], v_new))


def diff_out(o, subln_g, lambda_init, w_out):
    B, L = o.shape[0], o.shape[1]
    o = rms_norm(o, subln_g) * (1.0 - lambda_init)
    return o.reshape(B, L, N_HEADS * V_HEAD_DIM) @ w_out


def gmlp_project(h, w_in, ln_g, ln_b):
    z = jax.nn.gelu(h @ w_in, approximate=False)
    u, v = jnp.split(z, 2, axis=-1)
    return u, layer_norm(v, ln_g, ln_b)


def spatial_mix(v, w_s, b_s):
    B, L, _ = v.shape
    vc = v.reshape(B, L // CHUNK, CHUNK, GMLP_GROUPS, GMLP_GROUP_DIM)
    w = jnp.tril(w_s)
    s = jnp.einsum('gtr,bnrgc->bntgc', w, vc) + b_s.T[None, None, :, :, None]
    return s.reshape(B, L, GMLP_WIDTH)


def setup_inputs(seed: int = 0) -> dict:
    key = jax.random.key(seed)
    ks = jax.random.split(key, 40)
    f32 = jnp.float32
    n_pages = PAST_LEN // PAGE_SIZE
    n_used = DEC_BATCH * n_pages
    n_pool = n_used + max(1, n_used // 4)

    def w(k, shape, fan_in):
        return jax.random.normal(k, shape, f32) * fan_in ** -0.5

    def gain(k, shape):
        return 1.0 + 0.05 * jax.random.normal(k, shape, f32)

    qkv_w = 3 * N_HEADS * 2 * HEAD_DIM
    att_w = N_HEADS * V_HEAD_DIM
    page_table = jax.random.permutation(ks[0], n_pool)[:n_used].reshape(DEC_BATCH, n_pages).astype(jnp.int32)
    return {
        'x_prompt': jax.random.normal(ks[1], (BATCH, SEQ, D_MODEL), f32),
        'x_sample': jax.random.normal(ks[2], (DEC_BATCH, DEC_SEQ, D_MODEL), f32),
        'cache_k': jax.random.normal(ks[3], (N_ATTN_LAYERS, n_pool, PAGE_SIZE, N_HEADS, 2 * HEAD_DIM), f32),
        'cache_v': jax.random.normal(ks[4], (N_ATTN_LAYERS, n_pool, PAGE_SIZE, N_HEADS, V_HEAD_DIM), f32),
        'page_table': page_table,
        'norm_ff1': gain(ks[5], (DEPTH, D_MODEL)),
        'w_ff1_gate': w(ks[6], (DEPTH, D_MODEL, D_FF), D_MODEL),
        'w_ff1_up': w(ks[7], (DEPTH, D_MODEL, D_FF), D_MODEL),
        'w_ff1_down': w(ks[8], (DEPTH, D_FF, D_MODEL), D_FF),
        'norm_mix': gain(ks[9], (DEPTH, D_MODEL)),
        'norm_ff2': gain(ks[10], (DEPTH, D_MODEL)),
        'w_ff2_gate': w(ks[11], (DEPTH, D_MODEL, D_FF), D_MODEL),
        'w_ff2_up': w(ks[12], (DEPTH, D_MODEL, D_FF), D_MODEL),
        'w_ff2_down': w(ks[13], (DEPTH, D_FF, D_MODEL), D_FF),
        'w_attn_in': w(ks[14], (N_ATTN_LAYERS, D_MODEL, qkv_w), D_MODEL),
        'q_norm': gain(ks[15], (N_ATTN_LAYERS, HEAD_DIM)),
        'k_norm': gain(ks[16], (N_ATTN_LAYERS, HEAD_DIM)),
        'lambda_q1': 0.1 * jax.random.normal(ks[17], (N_ATTN_LAYERS, HEAD_DIM), f32),
        'lambda_k1': 0.1 * jax.random.normal(ks[18], (N_ATTN_LAYERS, HEAD_DIM), f32),
        'lambda_q2': 0.1 * jax.random.normal(ks[19], (N_ATTN_LAYERS, HEAD_DIM), f32),
        'lambda_k2': 0.1 * jax.random.normal(ks[20], (N_ATTN_LAYERS, HEAD_DIM), f32),
        'attn_subln': gain(ks[21], (N_ATTN_LAYERS, V_HEAD_DIM)),
        'w_attn_out': w(ks[22], (N_ATTN_LAYERS, att_w, D_MODEL), att_w),
        'w_gmlp_in': w(ks[23], (N_GMLP_LAYERS, D_MODEL, 2 * GMLP_WIDTH), D_MODEL),
        'gmlp_ln_g': gain(ks[24], (N_GMLP_LAYERS, GMLP_WIDTH)),
        'gmlp_ln_b': 0.02 * jax.random.normal(ks[25], (N_GMLP_LAYERS, GMLP_WIDTH), f32),
        'gmlp_w_s': w(ks[26], (N_GMLP_LAYERS, GMLP_GROUPS, CHUNK, CHUNK), CHUNK),
        'gmlp_b_s': 1.0 + 0.1 * jax.random.normal(ks[27], (N_GMLP_LAYERS, GMLP_GROUPS, CHUNK), f32),
        'w_gmlp_out': w(ks[28], (N_GMLP_LAYERS, GMLP_WIDTH, D_MODEL), GMLP_WIDTH),
    }


def reference(x_prompt, x_sample, cache_k, cache_v, page_table, norm_ff1, w_ff1_gate, w_ff1_up, w_ff1_down,
              norm_mix, norm_ff2, w_ff2_gate, w_ff2_up, w_ff2_down, w_attn_in, q_norm, k_norm,
              lambda_q1, lambda_k1, lambda_q2, lambda_k2, attn_subln, w_attn_out,
              w_gmlp_in, gmlp_ln_g, gmlp_ln_b, gmlp_w_s, gmlp_b_s, w_gmlp_out):
    B, S, _ = x_prompt.shape
    DB, T, _ = x_sample.shape
    n_past = page_table.shape[1] * cache_k.shape[2]
    pos_p = jnp.arange(S, dtype=jnp.int32)
    pos_s = n_past + jnp.arange(T, dtype=jnp.int32)
    t_pad = -(-T // CHUNK) * CHUNK

    xp, xs = x_prompt, x_sample
    new_k_p, new_v_p, new_k_s, new_v_s, gmlp_v_s = [], [], [], [], []
    for i in range(DEPTH):
        xp = xp + 0.5 * swiglu(rms_norm(xp, norm_ff1[i]), w_ff1_gate[i], w_ff1_up[i], w_ff1_down[i])
        xs = xs + 0.5 * swiglu(rms_norm(xs, norm_ff1[i]), w_ff1_gate[i], w_ff1_up[i], w_ff1_down[i])
        hp = rms_norm(xp, norm_mix[i])
        hs = rms_norm(xs, norm_mix[i])
        if i % N_MIXERS == 0:
            a = i // N_MIXERS
            lambda_init = 0.8 - 0.6 * math.exp(-0.3 * i)
            lam = diff_lambda(lambda_q1[a], lambda_k1[a], lambda_q2[a], lambda_k2[a], lambda_init)
            qp, kp, vp = diff_qkv(hp, w_attn_in[a], q_norm[a], k_norm[a], pos_p)
            mp = diff_out(diff_attn_prompt(qp, kp, vp, lam), attn_subln[a], lambda_init, w_attn_out[a])
            qs, ks_, vs_ = diff_qkv(hs, w_attn_in[a], q_norm[a], k_norm[a], pos_s)
            k_past = cache_k[a, page_table].reshape(DB, n_past, N_HEADS, 2, HEAD_DIM)
            v_past = cache_v[a, page_table].reshape(DB, n_past, N_HEADS, V_HEAD_DIM)
            ms = diff_out(diff_attn_sample(qs, ks_, vs_, k_past, v_past, lam), attn_subln[a], lambda_init, w_attn_out[a])
            new_k_p.append(kp.reshape(B, S, N_HEADS, 2 * HEAD_DIM))
            new_v_p.append(vp)
            new_k_s.append(ks_.reshape(DB, T, N_HEADS, 2 * HEAD_DIM))
            new_v_s.append(vs_)
        else:
            g = i // N_MIXERS
            up, vp = gmlp_project(hp, w_gmlp_in[g], gmlp_ln_g[g], gmlp_ln_b[g])
            mp = (up * spatial_mix(vp, gmlp_w_s[g], gmlp_b_s[g])) @ w_gmlp_out[g]
            us, vs_ = gmlp_project(hs, w_gmlp_in[g], gmlp_ln_g[g], gmlp_ln_b[g])
            vs_pad = jnp.pad(vs_, ((0, 0), (0, t_pad - T), (0, 0)))
            ms = (us * spatial_mix(vs_pad, gmlp_w_s[g], gmlp_b_s[g])[:, :T]) @ w_gmlp_out[g]
            gmlp_v_s.append(vs_)
        xp = xp + mp
        xs = xs + ms
        xp = xp + 0.5 * swiglu(rms_norm(xp, norm_ff2[i]), w_ff2_gate[i], w_ff2_up[i], w_ff2_down[i])
        xs = xs + 0.5 * swiglu(rms_norm(xs, norm_ff2[i]), w_ff2_gate[i], w_ff2_up[i], w_ff2_down[i])

    return (xp, xs, jnp.stack(new_k_p), jnp.stack(new_v_p), jnp.stack(new_k_s), jnp.stack(new_v_s), jnp.stack(gmlp_v_s))
```

```python
import functools
import math

import jax
import jax.numpy as jnp
from jax import lax
from jax.experimental import pallas as pl
from jax.experimental.pallas import tpu as pltpu

F32 = jnp.float32
BF16 = jnp.bfloat16

D_MODEL = 4096
D_FF = 11008
N_HEADS = 16
HEAD_DIM = 128
V_HEAD_DIM = 256
ROT_DIM = HEAD_DIM // 4
ROPE_THETA = 500000.0
CHUNK = 128
GMLP_GROUPS = 8
GROUP_DIM = D_MODEL // GMLP_GROUPS
EPS = 1e-6

TM = 1024
TS = 16
TR = TM + TS
TF = 256
TN = 512
NORM_ROWS = 208
NEG = -1e30
VMEM_LIMIT = 56 * 1024 * 1024

PAGES_PER_STEP = 4
HG = 8
TQ = 512
TK = 512


def _cparams(n_axes):
    return pltpu.CompilerParams(dimension_semantics=("arbitrary",) * n_axes,
                                vmem_limit_bytes=VMEM_LIMIT)


def _rms(x, g):
    return x * lax.rsqrt(jnp.mean(x * x, axis=-1, keepdims=True) + EPS) * g


def _ffn_kernel(xp_hbm, xs_ref, g_ref, wg_ref, wu_ref, wd_ref, outp_hbm, outs_ref,
                acc_ref, h_ref, sem):
    i = pl.program_id(0)
    j = pl.program_id(1)

    def x_copy():
        return pltpu.make_async_copy(xp_hbm.at[pl.ds(i * TM, TM), :],
                                     acc_ref.at[pl.ds(0, TM), :], sem.at[0])

    def out_copy():
        return pltpu.make_async_copy(acc_ref.at[pl.ds(0, TM), :],
                                     outp_hbm.at[pl.ds(i * TM, TM), :], sem.at[1])

    @pl.when(j == 0)
    def _():
        x_copy().start()
        acc_ref[TM:TR, :] = xs_ref[0]
        x_copy().wait()

        def body(r, c):
            rows = pl.ds(pl.multiple_of(r * NORM_ROWS, NORM_ROWS), NORM_ROWS)
            h_ref[rows, :] = _rms(acc_ref[rows, :], g_ref[...]).astype(BF16)
            return c
        lax.fori_loop(0, TR // NORM_ROWS, body, 0)

    h = h_ref[...]
    gate = jnp.dot(h, wg_ref[...], preferred_element_type=F32)
    up = jnp.dot(h, wu_ref[...], preferred_element_type=F32)
    act = (0.5 * (gate * jax.nn.sigmoid(gate)) * up).astype(BF16)
    for n in range(0, D_MODEL, TN):
        acc_ref[:, n:n + TN] += jnp.dot(act, wd_ref[:, n:n + TN], preferred_element_type=F32)

    @pl.when(j == pl.num_programs(1) - 1)
    def _():
        outs_ref[0] = acc_ref[TM:TR, :]
        out_copy().start()
        out_copy().wait()


def _ffn(xp, xs, g, wg, wu, wd, layer):
    nb = xp.shape[0] // TM
    return pl.pallas_call(
        _ffn_kernel,
        grid=(nb, D_FF // TF),
        in_specs=[
            pl.BlockSpec(memory_space=pl.ANY),
            pl.BlockSpec((1, TS, D_MODEL), lambda i, j: (i, 0, 0)),
            pl.BlockSpec((None, 1, D_MODEL), lambda i, j: (layer, 0, 0)),
            pl.BlockSpec((None, D_MODEL, TF), lambda i, j: (layer, 0, j)),
            pl.BlockSpec((None, D_MODEL, TF), lambda i, j: (layer, 0, j)),
            pl.BlockSpec((None, TF, D_MODEL), lambda i, j: (layer, j, 0)),
        ],
        out_specs=[
            pl.BlockSpec(memory_space=pl.ANY),
            pl.BlockSpec((1, TS, D_MODEL), lambda i, j: (i, 0, 0)),
        ],
        out_shape=[jax.ShapeDtypeStruct(xp.shape, F32), jax.ShapeDtypeStruct(xs.shape, F32)],
        scratch_shapes=[pltpu.VMEM((TR, D_MODEL), F32), pltpu.VMEM((TR, D_MODEL), BF16),
                        pltpu.SemaphoreType.DMA((2,))],
        compiler_params=_cparams(2),
        name="ffn",
    )(xp, xs, g.reshape(g.shape[0], 1, D_MODEL), wg, wu, wd)


def _norm_kernel(x_ref, g_ref, o_ref):
    o_ref[...] = _rms(x_ref[...], g_ref[...]).astype(BF16)


def _norm(x2d, g, layer, rows):
    return pl.pallas_call(
        _norm_kernel,
        grid=(x2d.shape[0] // rows,),
        in_specs=[pl.BlockSpec((rows, D_MODEL), lambda i: (i, 0)),
                  pl.BlockSpec((None, 1, D_MODEL), lambda i: (layer, 0, 0))],
        out_specs=pl.BlockSpec((rows, D_MODEL), lambda i: (i, 0)),
        out_shape=jax.ShapeDtypeStruct(x2d.shape, BF16),
        compiler_params=_cparams(1),
        name="norm",
    )(x2d, g.reshape(g.shape[0], 1, D_MODEL))


def _norm_rope(y, gain, cos, sin_lo, sin_hi):
    cols = []
    for c in range(y.shape[1] // HEAD_DIM):
        blk = _rms(y[:, c * HEAD_DIM:(c + 1) * HEAD_DIM], gain)
        blk = (blk * cos + pltpu.roll(blk, HEAD_DIM - ROT_DIM // 2, 1) * sin_lo
               + pltpu.roll(blk, ROT_DIM // 2, 1) * sin_hi)
        cols.append(blk)
    return jnp.concatenate(cols, axis=1)


def _gelu(y):
    return 0.5 * y * (1.0 + lax.erf(y * (2.0 ** -0.5)))


def _proj_kernel(*refs, epi, has_res, outs):
    it = iter(refs)
    lp_ref, ls_ref, w_ref = next(it), next(it), next(it)
    if epi == "norm_rope":
        gain_ref = next(it)
        tab_p = [next(it) for _ in range(3)]
        tab_s = [next(it) for _ in range(3)]
    if has_res:
        rp_ref, rs_ref = next(it), next(it)
    out_refs = {name: next(it) for name in outs}
    lhs_ref = next(it)

    @pl.when(pl.program_id(1) == 0)
    def _():
        lhs_ref[0:TM, :] = lp_ref[...]
        lhs_ref[TM:TR, :] = ls_ref[0]

    y = jnp.dot(lhs_ref[...], w_ref[...], preferred_element_type=F32)
    yp, ys = y[0:TM], y[TM:TR]
    if epi == "norm_rope":
        gain = gain_ref[...]
        yp = _norm_rope(yp, gain, *[t[...] for t in tab_p])
        ys = _norm_rope(ys, gain, *[t[...] for t in tab_s])
    elif epi == "gelu":
        yp, ys = _gelu(yp), _gelu(ys)
    if has_res:
        yp = rp_ref[...] + yp
        ys = rs_ref[0] + ys
    if "p_f32" in out_refs:
        out_refs["p_f32"][...] = yp
    if "p_bf16" in out_refs:
        out_refs["p_bf16"][...] = yp.astype(BF16)
    if "s_f32" in out_refs:
        out_refs["s_f32"][0] = ys
    if "s_bf16" in out_refs:
        out_refs["s_bf16"][0] = ys.astype(BF16)


def _proj(lp, ls, w, layer, col0, n_cols, *, epi="none", gain=None, tabs=None, res=None, outs):
    nb = lp.shape[0] // TM
    k_dim = lp.shape[1]
    cb0 = col0 // TN
    p_spec = pl.BlockSpec((TM, TN), lambda i, j: (i, j))
    s_spec = pl.BlockSpec((1, TS, TN), lambda i, j: (i, 0, j))
    in_specs = [pl.BlockSpec((TM, k_dim), lambda i, j: (i, 0)),
                pl.BlockSpec((1, TS, k_dim), lambda i, j: (i, 0, 0)),
                pl.BlockSpec((None, k_dim, TN), lambda i, j: (layer, 0, cb0 + j))]
    args = [lp, ls, w]
    if epi == "norm_rope":
        reps = lp.shape[0] // tabs[0][0].shape[0]
        pos_blocks = tabs[0][0].shape[0] // TM
        in_specs.append(pl.BlockSpec((None, 1, HEAD_DIM), lambda i, j: (layer, 0, 0)))
        args.append(gain.reshape(gain.shape[0], 1, HEAD_DIM))
        del reps
        for t in tabs[0]:
            in_specs.append(pl.BlockSpec((TM, HEAD_DIM), lambda i, j: (i % pos_blocks, 0)))
            args.append(t)
        for t in tabs[1]:
            in_specs.append(pl.BlockSpec((TS, HEAD_DIM), lambda i, j: (0, 0)))
            args.append(t)
    if res is not None:
        in_specs += [p_spec, s_spec]
        args += list(res)
    out_specs, out_shape = [], []
    for name in outs:
        dt = F32 if name.endswith("f32") else BF16
        if name.startswith("p_"):
            out_specs.append(p_spec)
            out_shape.append(jax.ShapeDtypeStruct((lp.shape[0], n_cols), dt))
        else:
            out_specs.append(s_spec)
            out_shape.append(jax.ShapeDtypeStruct((nb, TS, n_cols), dt))
    return pl.pallas_call(
        functools.partial(_proj_kernel, epi=epi, has_res=res is not None, outs=tuple(outs)),
        grid=(nb, n_cols // TN),
        in_specs=in_specs,
        out_specs=out_specs,
        out_shape=out_shape,
        scratch_shapes=[pltpu.VMEM((TR, k_dim), BF16)],
        compiler_params=_cparams(2),
        name="proj_" + epi + ("_res" if res is not None else ""),
    )(*args)


def _diff_lambda(lq1_ref, lk1_ref, lq2_ref, lk2_ref, lambda_init):
    s1 = jnp.sum(lq1_ref[...] * lk1_ref[...], axis=-1, keepdims=True)
    s2 = jnp.sum(lq2_ref[...] * lk2_ref[...], axis=-1, keepdims=True)
    return jnp.exp(s1) - jnp.exp(s2) + lambda_init


def _nt_dot(a, b):
    return lax.dot_general(a, b, (((1,), (1,)), ((), ())), preferred_element_type=F32)


def _attn_prompt_kernel(q_ref, k_ref, v_ref, lq1, lk1, lq2, lk2, g_ref, o_ref,
                        m_ref, l_ref, acc_ref, *, lambda_init):
    qi = pl.program_id(2)
    scale = HEAD_DIM ** -0.5
    m_ref[...] = jnp.full(m_ref.shape, NEG, F32)
    l_ref[...] = jnp.zeros(l_ref.shape, F32)
    acc_ref[...] = jnp.zeros(acc_ref.shape, F32)
    row = qi * TQ + lax.broadcasted_iota(jnp.int32, (TQ, TK), 0)
    col0 = lax.broadcasted_iota(jnp.int32, (TQ, TK), 1)

    def body(kj, carry):
        ks = pl.ds(pl.multiple_of(kj * TK, TK), TK)
        kblk = k_ref[ks, :]
        vblk = v_ref[ks, :]
        mask = (col0 + kj * TK) <= row
        for c in range(2):
            lanes = slice(c * HEAD_DIM, (c + 1) * HEAD_DIM)
            s = _nt_dot(q_ref[:, lanes], kblk[:, lanes]) * scale
            s = jnp.where(mask, s, NEG)
            m_old = m_ref[c]
            m_new = jnp.maximum(m_old, jnp.max(s, axis=-1, keepdims=True))
            alpha = jnp.exp(m_old - m_new)
            p = jnp.exp(s - m_new)
            l_ref[c] = alpha * l_ref[c] + jnp.sum(p, axis=-1, keepdims=True)
            acc_ref[c] = alpha * acc_ref[c] + jnp.dot(p.astype(BF16), vblk,
                                                      preferred_element_type=F32)
            m_ref[c] = m_new
        return carry

    lax.fori_loop(0, qi + 1, body, 0)
    lam = _diff_lambda(lq1, lk1, lq2, lk2, lambda_init)
    o = acc_ref[0] / l_ref[0] - lam * (acc_ref[1] / l_ref[1])
    o_ref[...] = (_rms(o, g_ref[...]) * (1.0 - lambda_init)).astype(BF16)


def _attn_prompt(q, k, v, lams, subln, layer, batch, seq, lambda_init):
    nq = seq // TQ
    lam_spec = pl.BlockSpec((None, 1, HEAD_DIM), lambda b, h, qi: (layer, 0, 0))
    return pl.pallas_call(
        functools.partial(_attn_prompt_kernel, lambda_init=lambda_init),
        grid=(batch, N_HEADS, nq),
        in_specs=[pl.BlockSpec((TQ, V_HEAD_DIM), lambda b, h, qi: (b * nq + qi, h)),
                  pl.BlockSpec((seq, V_HEAD_DIM), lambda b, h, qi: (b, h)),
                  pl.BlockSpec((seq, V_HEAD_DIM), lambda b, h, qi: (b, h)),
                  lam_spec, lam_spec, lam_spec, lam_spec,
                  pl.BlockSpec((None, 1, V_HEAD_DIM), lambda b, h, qi: (layer, 0, 0))],
        out_specs=pl.BlockSpec((TQ, V_HEAD_DIM), lambda b, h, qi: (b * nq + qi, h)),
        out_shape=jax.ShapeDtypeStruct(q.shape, BF16),
        scratch_shapes=[pltpu.VMEM((2, TQ, 1), F32), pltpu.VMEM((2, TQ, 1), F32),
                        pltpu.VMEM((2, TQ, V_HEAD_DIM), F32)],
        compiler_params=_cparams(3),
        name="attn_prompt",
    )(q, k, v, *[x.reshape(x.shape[0], 1, HEAD_DIM) for x in lams],
      subln.reshape(subln.shape[0], 1, V_HEAD_DIM))


def _attn_sample_kernel(pt_ref, wq_ref, *refs, lambda_init):
    del pt_ref
    k_refs = refs[0:PAGES_PER_STEP]
    v_refs = refs[PAGES_PER_STEP:2 * PAGES_PER_STEP]
    (kn_ref, vn_ref, lq1, lk1, lq2, lk2, g_ref, o_ref, m_ref, l_ref, acc_ref) = refs[2 * PAGES_PER_STEP:]
    s_idx = pl.program_id(1)
    scale = HEAD_DIM ** -0.5
    n_groups = N_HEADS // HG

    @pl.when(s_idx == 0)
    def _():
        m_ref[...] = jnp.full(m_ref.shape, NEG, F32)
        l_ref[...] = jnp.zeros(l_ref.shape, F32)
        acc_ref[...] = jnp.zeros(acc_ref.shape, F32)

    sub = lax.broadcasted_iota(jnp.int32, (HG, HEAD_DIM), 0)
    lane = lax.broadcasted_iota(jnp.int32, (HG, HEAD_DIM), 1)
    own = (sub == (lane // 4) % HG) & (lane < 2 * HG * 4)

    def update(kb, vb, hg, valid):
        n_tok = kb.shape[0]
        k2 = kb.reshape(n_tok * HG, V_HEAD_DIM).astype(BF16)
        v2 = vb.reshape(n_tok * HG, V_HEAD_DIM).astype(BF16)
        s = jnp.dot(k2, wq_ref[0, hg], preferred_element_type=F32) * scale
        s = s.reshape(n_tok, HG, HEAD_DIM)
        if valid is not None:
            s = jnp.where(valid, s, NEG)
        m_old = m_ref[hg]
        m_new = jnp.maximum(m_old, jnp.max(s, axis=0))
        alpha = jnp.exp(m_old - m_new)
        p = jnp.exp(s - m_new[None])
        l_ref[hg] = alpha * l_ref[hg] + jnp.sum(p, axis=0)
        m_ref[hg] = m_new
        p_own = jnp.where(own[None], p, 0.0).reshape(n_tok * HG, HEAD_DIM).astype(BF16)
        alpha_row = jnp.sum(jnp.where(own, alpha, 0.0), axis=0, keepdims=True)
        pv = lax.dot_general(v2, p_own, (((0,), (0,)), ((), ())), preferred_element_type=F32)
        acc_ref[hg] = acc_ref[hg] * alpha_row + pv

    for g in range(PAGES_PER_STEP):
        for hg in range(n_groups):
            heads = slice(hg * HG, (hg + 1) * HG)
            update(k_refs[g][:, heads, :], v_refs[g][:, heads, :], hg, None)

    @pl.when(s_idx == pl.num_programs(1) - 1)
    def _():
        n_new = kn_ref.shape[1]
        tok = lax.broadcasted_iota(jnp.int32, (n_new, HG, HEAD_DIM), 0)
        qpos = lax.broadcasted_iota(jnp.int32, (n_new, HG, HEAD_DIM), 2) % 4
        valid = tok <= qpos
        lam = _diff_lambda(lq1, lk1, lq2, lk2, lambda_init)
        for hg in range(n_groups):
            heads = slice(hg * HG, (hg + 1) * HG)
            update(kn_ref[0, :, heads, :], vn_ref[0, :, heads, :], hg, valid)
            l_row = jnp.sum(jnp.where(own, l_ref[hg], 0.0), axis=0, keepdims=True)
            o = acc_ref[hg] / jnp.where(l_row > 0.0, l_row, 1.0)
            o = o - lam * pltpu.roll(o, HEAD_DIM - HG * 4, 1)
            o = o * lax.rsqrt(jnp.mean(o * o, axis=0, keepdims=True) + EPS) * g_ref[...]
            o_ref[0, hg] = o * (1.0 - lambda_init)


def _attn_sample(page_table, wq, cache_k, cache_v, k_new, v_new, lams, subln, layer, lambda_init):
    n_b, n_pages = page_table.shape
    n_steps = n_pages // PAGES_PER_STEP
    n_groups = N_HEADS // HG
    page = cache_k.shape[2]

    def page_spec(g):
        return pl.BlockSpec((None, None, page, N_HEADS, V_HEAD_DIM),
                            lambda b, s, pt: (layer, pt[b * n_pages + s * PAGES_PER_STEP + g], 0, 0, 0))

    lam_spec = pl.BlockSpec((None, 1, HEAD_DIM), lambda b, s, pt: (layer, 0, 0))
    new_spec = pl.BlockSpec((1, 8, N_HEADS, V_HEAD_DIM), lambda b, s, pt: (b, 0, 0, 0))
    grid_spec = pltpu.PrefetchScalarGridSpec(
        num_scalar_prefetch=1,
        grid=(n_b, n_steps),
        in_specs=([pl.BlockSpec((1, n_groups, V_HEAD_DIM, HEAD_DIM), lambda b, s, pt: (b, 0, 0, 0))]
                  + [page_spec(g) for g in range(PAGES_PER_STEP)] * 2
                  + [new_spec, new_spec, lam_spec, lam_spec, lam_spec, lam_spec,
                     pl.BlockSpec((None, V_HEAD_DIM, 1), lambda b, s, pt: (layer, 0, 0))]),
        out_specs=pl.BlockSpec((1, n_groups, V_HEAD_DIM, HEAD_DIM), lambda b, s, pt: (b, 0, 0, 0)),
        scratch_shapes=[pltpu.VMEM((n_groups, HG, HEAD_DIM), F32),
                        pltpu.VMEM((n_groups, HG, HEAD_DIM), F32),
                        pltpu.VMEM((n_groups, V_HEAD_DIM, HEAD_DIM), F32)],
    )
    return pl.pallas_call(
        functools.partial(_attn_sample_kernel, lambda_init=lambda_init),
        grid_spec=grid_spec,
        out_shape=jax.ShapeDtypeStruct((n_b, n_groups, V_HEAD_DIM, HEAD_DIM), F32),
        compiler_params=_cparams(2),
        name="attn_sample",
    )(page_table.reshape(-1), wq, *([cache_k] * PAGES_PER_STEP), *([cache_v] * PAGES_PER_STEP),
      k_new, v_new, *[x.reshape(x.shape[0], 1, HEAD_DIM) for x in lams],
      subln.reshape(subln.shape[0], V_HEAD_DIM, 1))


def _sgu_kernel(zvp_ref, zvs_ref, up_ref, us_ref, lng_ref, lnb_ref, ws_ref, bs_ref,
                gp_ref, gs_ref, vs_out_ref, sum_ref, sq_ref):
    phase = pl.program_id(1)
    grp = pl.program_id(2)
    zp = zvp_ref[...]
    zs = zvs_ref[0]

    @pl.when((phase == 0) & (grp == 0))
    def _():
        sum_ref[...] = jnp.zeros(sum_ref.shape, F32)
        sq_ref[...] = jnp.zeros(sq_ref.shape, F32)

    @pl.when(phase == 0)
    def _():
        sum_ref[0:TM] += jnp.sum(zp, axis=-1, keepdims=True)
        sum_ref[TM:TR] += jnp.sum(zs, axis=-1, keepdims=True)
        sq_ref[0:TM] += jnp.sum(zp * zp, axis=-1, keepdims=True)
        sq_ref[TM:TR] += jnp.sum(zs * zs, axis=-1, keepdims=True)

    @pl.when(phase == 1)
    def _():
        inv_n = 1.0 / D_MODEL
        mu = sum_ref[...] * inv_n
        var = sq_ref[...] * inv_n - mu * mu
        rstd = lax.rsqrt(var + EPS)
        vp = (zp - mu[0:TM]) * rstd[0:TM] * lng_ref[...] + lnb_ref[...]
        vs = (zs - mu[TM:TR]) * rstd[TM:TR] * lng_ref[...] + lnb_ref[...]
        vs_out_ref[0] = vs
        r = lax.broadcasted_iota(jnp.int32, (CHUNK, CHUNK), 0)
        c = lax.broadcasted_iota(jnp.int32, (CHUNK, CHUNK), 1)
        w = jnp.where(c <= r, ws_ref[...], 0.0).astype(BF16)
        bias = bs_ref[...]
        for t in range(TM // CHUNK):
            rows = slice(t * CHUNK, (t + 1) * CHUNK)
            mix = jnp.dot(w, vp[rows].astype(BF16), preferred_element_type=F32) + bias
            gp_ref[rows, :] = (up_ref[rows, :].astype(F32) * mix).astype(BF16)
        vs_chunk = jnp.concatenate([vs, jnp.zeros((CHUNK - TS, GROUP_DIM), F32)], axis=0)
        keep = lax.broadcasted_iota(jnp.int32, (CHUNK, GROUP_DIM), 0) < 4
        vs_chunk = jnp.where(keep, vs_chunk, 0.0).astype(BF16)
        mix_s = jnp.dot(w, vs_chunk, preferred_element_type=F32) + bias
        gs_ref[0] = (us_ref[0].astype(F32) * mix_s[0:TS]).astype(BF16)


def _sgu(zv_p, zv_s, u_p, u_s, ln_g, ln_b, w_s, b_s, layer):
    nb = zv_p.shape[0] // TM
    p_spec = pl.BlockSpec((TM, GROUP_DIM), lambda i, ph, g: (i, g))
    s_spec = pl.BlockSpec((1, TS, GROUP_DIM), lambda i, ph, g: (i, 0, g))
    po_spec = pl.BlockSpec((TM, GROUP_DIM), lambda i, ph, g: (i, g * ph))
    so_spec = pl.BlockSpec((1, TS, GROUP_DIM), lambda i, ph, g: (i, 0, g * ph))
    vec_spec = pl.BlockSpec((None, 1, GROUP_DIM), lambda i, ph, g: (layer, 0, g))
    return pl.pallas_call(
        _sgu_kernel,
        grid=(nb, 2, GMLP_GROUPS),
        in_specs=[p_spec, s_spec, p_spec, s_spec, vec_spec, vec_spec,
                  pl.BlockSpec((None, None, CHUNK, CHUNK), lambda i, ph, g: (layer, g, 0, 0)),
                  pl.BlockSpec((None, None, CHUNK, 1), lambda i, ph, g: (layer, g, 0, 0))],
        out_specs=[po_spec, so_spec, so_spec],
        out_shape=[jax.ShapeDtypeStruct(zv_p.shape, BF16),
                   jax.ShapeDtypeStruct(zv_s.shape, BF16),
                   jax.ShapeDtypeStruct(zv_s.shape, F32)],
        scratch_shapes=[pltpu.VMEM((TR, 1), F32), pltpu.VMEM((TR, 1), F32)],
        compiler_params=_cparams(3),
        name="sgu",
    )(zv_p, zv_s, u_p, u_s, ln_g.reshape(ln_g.shape[0], 1, D_MODEL), ln_b.reshape(ln_b.shape[0], 1, D_MODEL),
      w_s, b_s.reshape(b_s.shape[0], GMLP_GROUPS, CHUNK, 1))


def _rope_tables(pos):
    half = ROT_DIM // 2
    inv = ROPE_THETA ** (-jnp.arange(half, dtype=F32) / half)
    ang = pos.astype(F32)[:, None] * inv[None, :]
    cos, sin = jnp.cos(ang), jnp.sin(ang)
    n = pos.shape[0]
    rest = HEAD_DIM - ROT_DIM
    cos_t = jnp.concatenate([cos, cos, jnp.ones((n, rest), F32)], axis=1)
    sin_lo = jnp.concatenate([-sin, jnp.zeros((n, HEAD_DIM - half), F32)], axis=1)
    sin_hi = jnp.concatenate([jnp.zeros((n, half), F32), sin, jnp.zeros((n, rest), F32)], axis=1)
    return cos_t, sin_lo, sin_hi


def _decode_query_matrix(q_s):
    n_b = q_s.shape[0]
    n_groups = N_HEADS // HG
    q = q_s[:, :4].reshape(n_b, 4, n_groups, HG, 2, HEAD_DIM)
    w = jnp.einsum("bqghcd,ce->bgcdehq", q, jnp.eye(2, dtype=F32))
    w = w.reshape(n_b, n_groups, 2 * HEAD_DIM, 2 * HG * 4)
    w = jnp.pad(w, ((0, 0), (0, 0), (0, 0), (0, HEAD_DIM - 2 * HG * 4)))
    return w.astype(BF16)


def kernel(x_prompt, x_sample, cache_k, cache_v, page_table, norm_ff1, w_ff1_gate, w_ff1_up, w_ff1_down,
           norm_mix, norm_ff2, w_ff2_gate, w_ff2_up, w_ff2_down, w_attn_in, q_norm, k_norm,
           lambda_q1, lambda_k1, lambda_q2, lambda_k2, attn_subln, w_attn_out,
           w_gmlp_in, gmlp_ln_g, gmlp_ln_b, gmlp_w_s, gmlp_b_s, w_gmlp_out):
    B, S, D = x_prompt.shape
    DB, T, _ = x_sample.shape
    depth = norm_ff1.shape[0]
    n_past = page_table.shape[1] * cache_k.shape[2]
    assert D == D_MODEL and (B * S) // TM == DB and S % TM == 0 and T <= 4

    xp = x_prompt.reshape(B * S, D)
    xs = jnp.pad(x_sample, ((0, 0), (0, TS - T), (0, 0)))

    bf = lambda w: w.astype(BF16)
    ff1 = (bf(w_ff1_gate), bf(w_ff1_up), bf(w_ff1_down))
    ff2 = (bf(w_ff2_gate), bf(w_ff2_up), bf(w_ff2_down))
    w_attn_in_b, w_attn_out_b = bf(w_attn_in), bf(w_attn_out)
    w_gmlp_in_b, w_gmlp_out_b = bf(w_gmlp_in), bf(w_gmlp_out)

    tabs = (_rope_tables(jnp.arange(S, dtype=jnp.int32)),
            _rope_tables(n_past + jnp.arange(TS, dtype=jnp.int32)))

    new_k_p, new_v_p, new_k_s, new_v_s, gmlp_v_s = [], [], [], [], []
    for i in range(depth):
        xp, xs = _ffn(xp, xs, norm_ff1, *ff1, i)
        hp = _norm(xp, norm_mix, i, 256)
        hs = _norm(xs.reshape(DB * TS, D), norm_mix, i, DB * TS).reshape(DB, TS, D)
        if i % 2 == 0:
            a = i // 2
            lambda_init = 0.8 - 0.6 * math.exp(-0.3 * i)
            lams = (lambda_q1, lambda_k1, lambda_q2, lambda_k2)
            qw = N_HEADS * 2 * HEAD_DIM
            q_p, q_s = _proj(hp, hs, w_attn_in_b, a, 0, qw, epi="norm_rope", gain=q_norm, tabs=tabs,
                             outs=("p_bf16", "s_f32"))
            k_pf, k_pb, k_s = _proj(hp, hs, w_attn_in_b, a, qw, qw, epi="norm_rope", gain=k_norm, tabs=tabs,
                                    outs=("p_f32", "p_bf16", "s_f32"))
            v_pf, v_pb, v_s = _proj(hp, hs, w_attn_in_b, a, 2 * qw, qw, outs=("p_f32", "p_bf16", "s_f32"))
            att_p = _attn_prompt(q_p, k_pb, v_pb, lams, attn_subln, a, B, S, lambda_init)
            o_s = _attn_sample(page_table, _decode_query_matrix(q_s), cache_k, cache_v,
                               k_s[:, :8].reshape(DB, 8, N_HEADS, V_HEAD_DIM),
                               v_s[:, :8].reshape(DB, 8, N_HEADS, V_HEAD_DIM),
                               lams, attn_subln, a, lambda_init)
            o_s = o_s[:, :, :, :HG * 4].reshape(DB, N_HEADS // HG, V_HEAD_DIM, HG, 4)
            o_s = o_s.transpose(0, 4, 1, 3, 2).reshape(DB, 4, D)
            att_s = jnp.pad(o_s, ((0, 0), (0, TS - 4), (0, 0))).astype(BF16)
            xp, xs = _proj(att_p, att_s, w_attn_out_b, a, 0, D, res=(xp, xs), outs=("p_f32", "s_f32"))
            new_k_p.append(k_pf.reshape(B, S, N_HEADS, 2 * HEAD_DIM))
            new_v_p.append(v_pf.reshape(B, S, N_HEADS, V_HEAD_DIM))
            new_k_s.append(k_s[:, :T].reshape(DB, T, N_HEADS, 2 * HEAD_DIM))
            new_v_s.append(v_s[:, :T].reshape(DB, T, N_HEADS, V_HEAD_DIM))
        else:
            g = i // 2
            u_p, u_s = _proj(hp, hs, w_gmlp_in_b, g, 0, D, epi="gelu", outs=("p_bf16", "s_bf16"))
            zv_p, zv_s = _proj(hp, hs, w_gmlp_in_b, g, D, D, epi="gelu", outs=("p_f32", "s_f32"))
            gate_p, gate_s, vn_s = _sgu(zv_p, zv_s, u_p, u_s, gmlp_ln_g, gmlp_ln_b, gmlp_w_s, gmlp_b_s, g)
            xp, xs = _proj(gate_p, gate_s, w_gmlp_out_b, g, 0, D, res=(xp, xs), outs=("p_f32", "s_f32"))
            gmlp_v_s.append(vn_s[:, :T])
        xp, xs = _ffn(xp, xs, norm_ff2, *ff2, i)

    return (xp.reshape(B, S, D), xs[:, :T], jnp.stack(new_k_p), jnp.stack(new_v_p),
            jnp.stack(new_k_s), jnp.stack(new_v_s), jnp.stack(gmlp_v_s))
```

```python
import functools
import math

import jax
import jax.numpy as jnp
from jax import lax
from jax.experimental import pallas as pl
from jax.experimental.pallas import tpu as pltpu

F32 = jnp.float32
BF16 = jnp.bfloat16

D_MODEL = 4096
D_FF = 11008
N_HEADS = 16
HEAD_DIM = 128
V_HEAD_DIM = 256
ROT_DIM = HEAD_DIM // 4
ROPE_THETA = 500000.0
CHUNK = 128
GMLP_GROUPS = 8
GROUP_DIM = D_MODEL // GMLP_GROUPS
EPS = 1e-6

TM = 1024
TS = 16
TR = TM + TS
TF = 256
TN = 512
SUB_N = 256
NORM_ROWS = 208
NEG = -1e30
VMEM_LIMIT = 56 * 1024 * 1024

PAGES_PER_STEP = 4
HG = 8
TQ = 512
TK = 512


def _cparams(n_axes):
    return pltpu.CompilerParams(dimension_semantics=("arbitrary",) * n_axes,
                                vmem_limit_bytes=VMEM_LIMIT)


def _rms(x, g):
    return x * lax.rsqrt(jnp.mean(x * x, axis=-1, keepdims=True) + EPS) * g


def _ffn_kernel(*refs, cast_next):
    if cast_next:
        (xp_hbm, xs_ref, g_ref, wg_ref, wu_ref, wd_ref, ng_ref, nu_ref, nd_ref,
         outp_hbm, outs_ref, cg_ref, cu_ref, cd_ref, acc_ref, h_ref, sem) = refs
        cg_ref[...] = ng_ref[...].astype(BF16)
        cu_ref[...] = nu_ref[...].astype(BF16)
        cd_ref[...] = nd_ref[...].astype(BF16)
    else:
        xp_hbm, xs_ref, g_ref, wg_ref, wu_ref, wd_ref, outp_hbm, outs_ref, acc_ref, h_ref, sem = refs
    i = pl.program_id(0)
    j = pl.program_id(1)
    n_chunks = D_MODEL // TN

    def x_copy():
        return pltpu.make_async_copy(xp_hbm.at[pl.ds(i * TM, TM), :],
                                     acc_ref.at[pl.ds(0, TM), :], sem.at[n_chunks])

    def out_copy(c):
        cols = pl.ds(c * TN, TN)
        return pltpu.make_async_copy(acc_ref.at[pl.ds(0, TM), cols],
                                     outp_hbm.at[pl.ds(i * TM, TM), cols], sem.at[c])

    @pl.when(j == 0)
    def _():
        x_copy().start()
        acc_ref[TM:TR, :] = xs_ref[0]
        x_copy().wait()

        def body(r, c):
            rows = pl.ds(pl.multiple_of(r * NORM_ROWS, NORM_ROWS), NORM_ROWS)
            h_ref[rows, :] = _rms(acc_ref[rows, :], g_ref[...]).astype(BF16)
            return c
        lax.fori_loop(0, TR // NORM_ROWS, body, 0)

    def step(write_back):
        h = h_ref[...]
        gate = jnp.dot(h, wg_ref[...], preferred_element_type=F32)
        up = jnp.dot(h, wu_ref[...], preferred_element_type=F32)
        act = (0.5 * (gate * jax.nn.sigmoid(gate)) * up).astype(BF16)
        for c in range(n_chunks):
            cols = slice(c * TN, (c + 1) * TN)
            acc_ref[:, cols] += jnp.dot(act, wd_ref[:, cols], preferred_element_type=F32)
            if write_back:
                out_copy(c).start()
        if write_back:
            outs_ref[0] = acc_ref[TM:TR, :]
            for c in range(n_chunks):
                out_copy(c).wait()

    last = pl.num_programs(1) - 1
    pl.when(j < last)(functools.partial(step, False))
    pl.when(j == last)(functools.partial(step, True))


def _ffn(xp, xs, g, wg, wu, wd, layer, nxt=None):
    nb = xp.shape[0] // TM
    nj = D_FF // TF
    rows_g = D_MODEL // nb
    in_specs = [
        pl.BlockSpec(memory_space=pl.ANY),
        pl.BlockSpec((1, TS, D_MODEL), lambda i, j: (i, 0, 0)),
        pl.BlockSpec((None, 1, D_MODEL), lambda i, j: (layer, 0, 0)),
        pl.BlockSpec((D_MODEL, TF), lambda i, j: (0, j)),
        pl.BlockSpec((D_MODEL, TF), lambda i, j: (0, j)),
        pl.BlockSpec((TF, D_MODEL), lambda i, j: (j, 0)),
    ]
    out_specs = [
        pl.BlockSpec(memory_space=pl.ANY),
        pl.BlockSpec((1, TS, D_MODEL), lambda i, j: (i, 0, 0)),
    ]
    out_shape = [jax.ShapeDtypeStruct(xp.shape, F32), jax.ShapeDtypeStruct(xs.shape, F32)]
    args = [xp, xs, g.reshape(g.shape[0], 1, D_MODEL), wg, wu, wd]
    if nxt is not None:
        ng, nu, nd, nl = nxt
        in_specs += [pl.BlockSpec((None, rows_g, TF), lambda i, j: (nl, i, j)),
                     pl.BlockSpec((None, rows_g, TF), lambda i, j: (nl, i, j)),
                     pl.BlockSpec((None, TF, rows_g), lambda i, j: (nl, j, i))]
        out_specs += [pl.BlockSpec((rows_g, TF), lambda i, j: (i, j)),
                      pl.BlockSpec((rows_g, TF), lambda i, j: (i, j)),
                      pl.BlockSpec((TF, rows_g), lambda i, j: (j, i))]
        out_shape += [jax.ShapeDtypeStruct((D_MODEL, D_FF), BF16), jax.ShapeDtypeStruct((D_MODEL, D_FF), BF16),
                      jax.ShapeDtypeStruct((D_FF, D_MODEL), BF16)]
        args += [ng, nu, nd]
    res = pl.pallas_call(
        functools.partial(_ffn_kernel, cast_next=nxt is not None),
        grid=(nb, nj),
        in_specs=in_specs,
        out_specs=out_specs,
        out_shape=out_shape,
        scratch_shapes=[pltpu.VMEM((TR, D_MODEL), F32), pltpu.VMEM((TR, D_MODEL), BF16),
                        pltpu.SemaphoreType.DMA((D_MODEL // TN + 1,))],
        compiler_params=_cparams(2),
        name="ffn",
    )(*args)
    return res[0], res[1], tuple(res[2:])


def _norm_kernel(x_ref, g_ref, o_ref):
    o_ref[...] = _rms(x_ref[...], g_ref[...]).astype(BF16)


def _norm(x2d, g, layer, rows):
    return pl.pallas_call(
        _norm_kernel,
        grid=(x2d.shape[0] // rows,),
        in_specs=[pl.BlockSpec((rows, D_MODEL), lambda i: (i, 0)),
                  pl.BlockSpec((None, 1, D_MODEL), lambda i: (layer, 0, 0))],
        out_specs=pl.BlockSpec((rows, D_MODEL), lambda i: (i, 0)),
        out_shape=jax.ShapeDtypeStruct(x2d.shape, BF16),
        compiler_params=_cparams(1),
        name="norm",
    )(x2d, g.reshape(g.shape[0], 1, D_MODEL))


def _norm_rope(y, gain, cos, sin_lo, sin_hi):
    cols = []
    for c in range(y.shape[1] // HEAD_DIM):
        blk = _rms(y[:, c * HEAD_DIM:(c + 1) * HEAD_DIM], gain)
        blk = (blk * cos + pltpu.roll(blk, HEAD_DIM - ROT_DIM // 2, 1) * sin_lo
               + pltpu.roll(blk, ROT_DIM // 2, 1) * sin_hi)
        cols.append(blk)
    return jnp.concatenate(cols, axis=1)


def _gelu(y):
    return 0.5 * y * (1.0 + lax.erf(y * (2.0 ** -0.5)))


def _proj_kernel(*refs, epi, has_res, outs):
    it = iter(refs)
    lp_ref, ls_ref, w_ref = next(it), next(it), next(it)
    if epi == "norm_rope":
        gain_ref = next(it)
        tab_p = [next(it) for _ in range(3)]
        tab_s = [next(it) for _ in range(3)]
    if has_res:
        rp_ref, rs_ref = next(it), next(it)
    out_refs = {name: next(it) for name in outs}
    lhs_ref = next(it)

    @pl.when(pl.program_id(1) == 0)
    def _():
        lhs_ref[0:TM, :] = lp_ref[...]
        lhs_ref[TM:TR, :] = ls_ref[0]

    sub = TN if epi == "none" else SUB_N
    for c0 in range(0, TN, sub):
        cols = slice(c0, c0 + sub)
        y = jnp.dot(lhs_ref[...], w_ref[:, cols], preferred_element_type=F32)
        yp, ys = y[0:TM], y[TM:TR]
        if epi == "norm_rope":
            gain = gain_ref[...]
            yp = _norm_rope(yp, gain, *[t[...] for t in tab_p])
            ys = _norm_rope(ys, gain, *[t[...] for t in tab_s])
        elif epi == "gelu":
            yp, ys = _gelu(yp), _gelu(ys)
        if has_res:
            yp = rp_ref[:, cols] + yp
            ys = rs_ref[0, :, cols] + ys
        if "p_f32" in out_refs:
            out_refs["p_f32"][:, cols] = yp
        if "p_bf16" in out_refs:
            out_refs["p_bf16"][:, cols] = yp.astype(BF16)
        if "s_f32" in out_refs:
            out_refs["s_f32"][0, :, cols] = ys
        if "s_bf16" in out_refs:
            out_refs["s_bf16"][0, :, cols] = ys.astype(BF16)


def _proj(lp, ls, w, layer, col0, n_cols, *, epi="none", gain=None, tabs=None, res=None, outs):
    nb = lp.shape[0] // TM
    k_dim = lp.shape[1]
    cb0 = col0 // TN
    p_spec = pl.BlockSpec((TM, TN), lambda i, j: (i, j))
    s_spec = pl.BlockSpec((1, TS, TN), lambda i, j: (i, 0, j))
    in_specs = [pl.BlockSpec((TM, k_dim), lambda i, j: (i, 0)),
                pl.BlockSpec((1, TS, k_dim), lambda i, j: (i, 0, 0)),
                pl.BlockSpec((None, k_dim, TN), lambda i, j: (layer, 0, cb0 + j))]
    args = [lp, ls, w]
    if epi == "norm_rope":
        pos_blocks = tabs[0][0].shape[0] // TM
        in_specs.append(pl.BlockSpec((None, 1, HEAD_DIM), lambda i, j: (layer, 0, 0)))
        args.append(gain.reshape(gain.shape[0], 1, HEAD_DIM))
        for t in tabs[0]:
            in_specs.append(pl.BlockSpec((TM, HEAD_DIM), lambda i, j: (i % pos_blocks, 0)))
            args.append(t)
        for t in tabs[1]:
            in_specs.append(pl.BlockSpec((TS, HEAD_DIM), lambda i, j: (0, 0)))
            args.append(t)
    if res is not None:
        in_specs += [p_spec, s_spec]
        args += list(res)
    out_specs, out_shape = [], []
    for name in outs:
        dt = F32 if name.endswith("f32") else BF16
        if name.startswith("p_"):
            out_specs.append(p_spec)
            out_shape.append(jax.ShapeDtypeStruct((lp.shape[0], n_cols), dt))
        else:
            out_specs.append(s_spec)
            out_shape.append(jax.ShapeDtypeStruct((nb, TS, n_cols), dt))
    return pl.pallas_call(
        functools.partial(_proj_kernel, epi=epi, has_res=res is not None, outs=tuple(outs)),
        grid=(nb, n_cols // TN),
        in_specs=in_specs,
        out_specs=out_specs,
        out_shape=out_shape,
        scratch_shapes=[pltpu.VMEM((TR, k_dim), BF16)],
        compiler_params=_cparams(2),
        name="proj_" + epi + ("_res" if res is not None else ""),
    )(*args)


def _diff_lambda(lq1_ref, lk1_ref, lq2_ref, lk2_ref, lambda_init):
    s1 = jnp.sum(lq1_ref[...] * lk1_ref[...], axis=-1, keepdims=True)
    s2 = jnp.sum(lq2_ref[...] * lk2_ref[...], axis=-1, keepdims=True)
    return jnp.exp(s1) - jnp.exp(s2) + lambda_init


def _nt_dot(a, b):
    return lax.dot_general(a, b, (((1,), (1,)), ((), ())), preferred_element_type=F32)


def _attn_prompt_kernel(q_ref, k_ref, v_ref, lq1, lk1, lq2, lk2, g_ref, o_ref,
                        s_ref, m_ref, l_ref, acc_ref, *, lambda_init):
    qi = pl.program_id(2)
    scale = HEAD_DIM ** -0.5
    m_ref[...] = jnp.full(m_ref.shape, NEG, F32)
    l_ref[...] = jnp.zeros(l_ref.shape, F32)
    acc_ref[...] = jnp.zeros(acc_ref.shape, F32)

    def scores(kj, c):
        ks = pl.ds(pl.multiple_of(kj * TK, TK), TK)
        lanes = slice(c * HEAD_DIM, (c + 1) * HEAD_DIM)
        return _nt_dot(q_ref[:, lanes], k_ref[ks, lanes]) * scale

    def keep(kj, c, s):
        s_ref[c, kj] = s
        m_ref[c] = jnp.maximum(m_ref[c], jnp.max(s, axis=-1, keepdims=True))

    def pass1(kj, carry):
        for c in range(2):
            keep(kj, c, scores(kj, c))
        return carry

    lax.fori_loop(0, qi, pass1, 0)
    causal = (lax.broadcasted_iota(jnp.int32, (TQ, TK), 1)
              <= lax.broadcasted_iota(jnp.int32, (TQ, TK), 0))
    for c in range(2):
        keep(qi, c, jnp.where(causal, scores(qi, c), NEG))

    def pass2(kj, carry):
        vblk = v_ref[pl.ds(pl.multiple_of(kj * TK, TK), TK), :]
        for c in range(2):
            p = jnp.exp(s_ref[c, kj] - m_ref[c])
            l_ref[c] += jnp.sum(p, axis=-1, keepdims=True)
            acc_ref[c] += jnp.dot(p.astype(BF16), vblk, preferred_element_type=F32)
        return carry

    lax.fori_loop(0, qi + 1, pass2, 0)
    lam = _diff_lambda(lq1, lk1, lq2, lk2, lambda_init)
    o = acc_ref[0] / l_ref[0] - lam * (acc_ref[1] / l_ref[1])
    o_ref[...] = (_rms(o, g_ref[...]) * (1.0 - lambda_init)).astype(BF16)


def _attn_prompt(q, k, v, lams, subln, layer, batch, seq, lambda_init):
    nq = seq // TQ
    lam_spec = pl.BlockSpec((None, 1, HEAD_DIM), lambda b, h, qi: (layer, 0, 0))
    return pl.pallas_call(
        functools.partial(_attn_prompt_kernel, lambda_init=lambda_init),
        grid=(batch, N_HEADS, nq),
        in_specs=[pl.BlockSpec((TQ, V_HEAD_DIM), lambda b, h, qi: (b * nq + qi, h)),
                  pl.BlockSpec((seq, V_HEAD_DIM), lambda b, h, qi: (b, h)),
                  pl.BlockSpec((seq, V_HEAD_DIM), lambda b, h, qi: (b, h)),
                  lam_spec, lam_spec, lam_spec, lam_spec,
                  pl.BlockSpec((None, 1, V_HEAD_DIM), lambda b, h, qi: (layer, 0, 0))],
        out_specs=pl.BlockSpec((TQ, V_HEAD_DIM), lambda b, h, qi: (b * nq + qi, h)),
        out_shape=jax.ShapeDtypeStruct(q.shape, BF16),
        scratch_shapes=[pltpu.VMEM((2, seq // TK, TQ, TK), F32),
                        pltpu.VMEM((2, TQ, 1), F32), pltpu.VMEM((2, TQ, 1), F32),
                        pltpu.VMEM((2, TQ, V_HEAD_DIM), F32)],
        compiler_params=_cparams(3),
        name="attn_prompt",
    )(q, k, v, *[x.reshape(x.shape[0], 1, HEAD_DIM) for x in lams],
      subln.reshape(subln.shape[0], 1, V_HEAD_DIM))


def _attn_sample_kernel(pt_ref, wq_ref, *refs, lambda_init):
    del pt_ref
    k_refs = refs[0:PAGES_PER_STEP]
    v_refs = refs[PAGES_PER_STEP:2 * PAGES_PER_STEP]
    (kn_ref, vn_ref, lq1, lk1, lq2, lk2, g_ref, o_ref, m_ref, l_ref, acc_ref) = refs[2 * PAGES_PER_STEP:]
    s_idx = pl.program_id(1)
    scale = HEAD_DIM ** -0.5
    n_groups = N_HEADS // HG

    @pl.when(s_idx == 0)
    def _():
        m_ref[...] = jnp.full(m_ref.shape, NEG, F32)
        l_ref[...] = jnp.zeros(l_ref.shape, F32)
        acc_ref[...] = jnp.zeros(acc_ref.shape, F32)

    sub = lax.broadcasted_iota(jnp.int32, (HG, HEAD_DIM), 0)
    lane = lax.broadcasted_iota(jnp.int32, (HG, HEAD_DIM), 1)
    own = (sub == (lane // 4) % HG) & (lane < 2 * HG * 4)

    def update(kb, vb, hg, valid):
        n_tok = kb.shape[0]
        k2 = kb.reshape(n_tok * HG, V_HEAD_DIM).astype(BF16)
        v2 = vb.reshape(n_tok * HG, V_HEAD_DIM).astype(BF16)
        s = jnp.dot(k2, wq_ref[0, hg], preferred_element_type=F32) * scale
        s = s.reshape(n_tok, HG, HEAD_DIM)
        if valid is not None:
            s = jnp.where(valid, s, NEG)
        m_old = m_ref[hg]
        m_new = jnp.maximum(m_old, jnp.max(s, axis=0))
        alpha = jnp.exp(m_old - m_new)
        p = jnp.exp(s - m_new[None])
        l_ref[hg] = alpha * l_ref[hg] + jnp.sum(p, axis=0)
        m_ref[hg] = m_new
        p_own = jnp.where(own[None], p, 0.0).reshape(n_tok * HG, HEAD_DIM).astype(BF16)
        alpha_row = jnp.sum(jnp.where(own, alpha, 0.0), axis=0, keepdims=True)
        pv = lax.dot_general(v2, p_own, (((0,), (0,)), ((), ())), preferred_element_type=F32)
        acc_ref[hg] = acc_ref[hg] * alpha_row + pv

    for g in range(PAGES_PER_STEP):
        for hg in range(n_groups):
            heads = slice(hg * HG, (hg + 1) * HG)
            update(k_refs[g][:, heads, :], v_refs[g][:, heads, :], hg, None)

    @pl.when(s_idx == pl.num_programs(1) - 1)
    def _():
        n_new = kn_ref.shape[1]
        tok = lax.broadcasted_iota(jnp.int32, (n_new, HG, HEAD_DIM), 0)
        qpos = lax.broadcasted_iota(jnp.int32, (n_new, HG, HEAD_DIM), 2) % 4
        valid = tok <= qpos
        lam = _diff_lambda(lq1, lk1, lq2, lk2, lambda_init)
        for hg in range(n_groups):
            heads = slice(hg * HG, (hg + 1) * HG)
            update(kn_ref[0, :, heads, :], vn_ref[0, :, heads, :], hg, valid)
            l_row = jnp.sum(jnp.where(own, l_ref[hg], 0.0), axis=0, keepdims=True)
            o = acc_ref[hg] / jnp.where(l_row > 0.0, l_row, 1.0)
            o = o - lam * pltpu.roll(o, HEAD_DIM - HG * 4, 1)
            o = o * lax.rsqrt(jnp.mean(o * o, axis=0, keepdims=True) + EPS) * g_ref[...]
            o_ref[0, hg] = o * (1.0 - lambda_init)


def _attn_sample(page_table, wq, cache_k, cache_v, k_new, v_new, lams, subln, layer, lambda_init):
    n_b, n_pages = page_table.shape
    n_steps = n_pages // PAGES_PER_STEP
    n_groups = N_HEADS // HG
    page = cache_k.shape[2]

    def page_spec(g):
        return pl.BlockSpec((None, None, page, N_HEADS, V_HEAD_DIM),
                            lambda b, s, pt: (layer, pt[b * n_pages + s * PAGES_PER_STEP + g], 0, 0, 0))

    lam_spec = pl.BlockSpec((None, 1, HEAD_DIM), lambda b, s, pt: (layer, 0, 0))
    new_spec = pl.BlockSpec((1, 8, N_HEADS, V_HEAD_DIM), lambda b, s, pt: (b, 0, 0, 0))
    grid_spec = pltpu.PrefetchScalarGridSpec(
        num_scalar_prefetch=1,
        grid=(n_b, n_steps),
        in_specs=([pl.BlockSpec((1, n_groups, V_HEAD_DIM, HEAD_DIM), lambda b, s, pt: (b, 0, 0, 0))]
                  + [page_spec(g) for g in range(PAGES_PER_STEP)] * 2
                  + [new_spec, new_spec, lam_spec, lam_spec, lam_spec, lam_spec,
                     pl.BlockSpec((None, V_HEAD_DIM, 1), lambda b, s, pt: (layer, 0, 0))]),
        out_specs=pl.BlockSpec((1, n_groups, V_HEAD_DIM, HEAD_DIM), lambda b, s, pt: (b, 0, 0, 0)),
        scratch_shapes=[pltpu.VMEM((n_groups, HG, HEAD_DIM), F32),
                        pltpu.VMEM((n_groups, HG, HEAD_DIM), F32),
                        pltpu.VMEM((n_groups, V_HEAD_DIM, HEAD_DIM), F32)],
    )
    return pl.pallas_call(
        functools.partial(_attn_sample_kernel, lambda_init=lambda_init),
        grid_spec=grid_spec,
        out_shape=jax.ShapeDtypeStruct((n_b, n_groups, V_HEAD_DIM, HEAD_DIM), F32),
        compiler_params=_cparams(2),
        name="attn_sample",
    )(page_table.reshape(-1), wq, *([cache_k] * PAGES_PER_STEP), *([cache_v] * PAGES_PER_STEP),
      k_new, v_new, *[x.reshape(x.shape[0], 1, HEAD_DIM) for x in lams],
      subln.reshape(subln.shape[0], V_HEAD_DIM, 1))


def _sgu_kernel(zvp_ref, zvs_ref, up_ref, us_ref, lng_ref, lnb_ref, ws_ref, bs_ref,
                gp_ref, gs_ref, vs_out_ref, sum_ref, sq_ref):
    phase = pl.program_id(1)
    grp = pl.program_id(2)
    zp = zvp_ref[...]
    zs = zvs_ref[0]

    @pl.when((phase == 0) & (grp == 0))
    def _():
        sum_ref[...] = jnp.zeros(sum_ref.shape, F32)
        sq_ref[...] = jnp.zeros(sq_ref.shape, F32)

    @pl.when(phase == 0)
    def _():
        sum_ref[0:TM] += jnp.sum(zp, axis=-1, keepdims=True)
        sum_ref[TM:TR] += jnp.sum(zs, axis=-1, keepdims=True)
        sq_ref[0:TM] += jnp.sum(zp * zp, axis=-1, keepdims=True)
        sq_ref[TM:TR] += jnp.sum(zs * zs, axis=-1, keepdims=True)

    @pl.when(phase == 1)
    def _():
        inv_n = 1.0 / D_MODEL
        mu = sum_ref[...] * inv_n
        var = sq_ref[...] * inv_n - mu * mu
        rstd = lax.rsqrt(var + EPS)
        vp = (zp - mu[0:TM]) * rstd[0:TM] * lng_ref[...] + lnb_ref[...]
        vs = (zs - mu[TM:TR]) * rstd[TM:TR] * lng_ref[...] + lnb_ref[...]
        vs_out_ref[0] = vs
        r = lax.broadcasted_iota(jnp.int32, (CHUNK, CHUNK), 0)
        c = lax.broadcasted_iota(jnp.int32, (CHUNK, CHUNK), 1)
        w = jnp.where(c <= r, ws_ref[...], 0.0).astype(BF16)
        bias = bs_ref[...]
        for t in range(TM // CHUNK):
            rows = slice(t * CHUNK, (t + 1) * CHUNK)
            mix = jnp.dot(w, vp[rows].astype(BF16), preferred_element_type=F32) + bias
            gp_ref[rows, :] = (up_ref[rows, :].astype(F32) * mix).astype(BF16)
        vs_chunk = jnp.concatenate([vs, jnp.zeros((CHUNK - TS, GROUP_DIM), F32)], axis=0)
        keep = lax.broadcasted_iota(jnp.int32, (CHUNK, GROUP_DIM), 0) < 4
        vs_chunk = jnp.where(keep, vs_chunk, 0.0).astype(BF16)
        mix_s = jnp.dot(w, vs_chunk, preferred_element_type=F32) + bias
        gs_ref[0] = (us_ref[0].astype(F32) * mix_s[0:TS]).astype(BF16)


def _sgu(zv_p, zv_s, u_p, u_s, ln_g, ln_b, w_s, b_s, layer):
    nb = zv_p.shape[0] // TM
    p_spec = pl.BlockSpec((TM, GROUP_DIM), lambda i, ph, g: (i, g))
    s_spec = pl.BlockSpec((1, TS, GROUP_DIM), lambda i, ph, g: (i, 0, g))
    po_spec = pl.BlockSpec((TM, GROUP_DIM), lambda i, ph, g: (i, g * ph))
    so_spec = pl.BlockSpec((1, TS, GROUP_DIM), lambda i, ph, g: (i, 0, g * ph))
    vec_spec = pl.BlockSpec((None, 1, GROUP_DIM), lambda i, ph, g: (layer, 0, g))
    return pl.pallas_call(
        _sgu_kernel,
        grid=(nb, 2, GMLP_GROUPS),
        in_specs=[p_spec, s_spec, p_spec, s_spec, vec_spec, vec_spec,
                  pl.BlockSpec((None, None, CHUNK, CHUNK), lambda i, ph, g: (layer, g, 0, 0)),
                  pl.BlockSpec((None, None, CHUNK, 1), lambda i, ph, g: (layer, g, 0, 0))],
        out_specs=[po_spec, so_spec, so_spec],
        out_shape=[jax.ShapeDtypeStruct(zv_p.shape, BF16),
                   jax.ShapeDtypeStruct(zv_s.shape, BF16),
                   jax.ShapeDtypeStruct(zv_s.shape, F32)],
        scratch_shapes=[pltpu.VMEM((TR, 1), F32), pltpu.VMEM((TR, 1), F32)],
        compiler_params=_cparams(3),
        name="sgu",
    )(zv_p, zv_s, u_p, u_s, ln_g.reshape(ln_g.shape[0], 1, D_MODEL), ln_b.reshape(ln_b.shape[0], 1, D_MODEL),
      w_s, b_s.reshape(b_s.shape[0], GMLP_GROUPS, CHUNK, 1))


def _rope_tables(pos):
    half = ROT_DIM // 2
    inv = ROPE_THETA ** (-jnp.arange(half, dtype=F32) / half)
    ang = pos.astype(F32)[:, None] * inv[None, :]
    cos, sin = jnp.cos(ang), jnp.sin(ang)
    n = pos.shape[0]
    rest = HEAD_DIM - ROT_DIM
    cos_t = jnp.concatenate([cos, cos, jnp.ones((n, rest), F32)], axis=1)
    sin_lo = jnp.concatenate([-sin, jnp.zeros((n, HEAD_DIM - half), F32)], axis=1)
    sin_hi = jnp.concatenate([jnp.zeros((n, half), F32), sin, jnp.zeros((n, rest), F32)], axis=1)
    return cos_t, sin_lo, sin_hi


def _decode_query_matrix(q_s):
    n_b = q_s.shape[0]
    n_groups = N_HEADS // HG
    q = q_s[:, :4].reshape(n_b, 4, n_groups, HG, 2, HEAD_DIM)
    w = jnp.einsum("bqghcd,ce->bgcdehq", q, jnp.eye(2, dtype=F32))
    w = w.reshape(n_b, n_groups, 2 * HEAD_DIM, 2 * HG * 4)
    w = jnp.pad(w, ((0, 0), (0, 0), (0, 0), (0, HEAD_DIM - 2 * HG * 4)))
    return w.astype(BF16)


def kernel(x_prompt, x_sample, cache_k, cache_v, page_table, norm_ff1, w_ff1_gate, w_ff1_up, w_ff1_down,
           norm_mix, norm_ff2, w_ff2_gate, w_ff2_up, w_ff2_down, w_attn_in, q_norm, k_norm,
           lambda_q1, lambda_k1, lambda_q2, lambda_k2, attn_subln, w_attn_out,
           w_gmlp_in, gmlp_ln_g, gmlp_ln_b, gmlp_w_s, gmlp_b_s, w_gmlp_out):
    B, S, D = x_prompt.shape
    DB, T, _ = x_sample.shape
    depth = norm_ff1.shape[0]
    n_past = page_table.shape[1] * cache_k.shape[2]
    assert D == D_MODEL and (B * S) // TM == DB and S % TM == 0 and T <= 4

    xp = x_prompt.reshape(B * S, D)
    xs = jnp.pad(x_sample, ((0, 0), (0, TS - T), (0, 0)))

    bf = lambda w: w.astype(BF16)
    ff_f32 = []
    for i in range(depth):
        ff_f32.append((norm_ff1, w_ff1_gate, w_ff1_up, w_ff1_down, i))
        ff_f32.append((norm_ff2, w_ff2_gate, w_ff2_up, w_ff2_down, i))
    ff_state = {"n": 0, "w": tuple(bf(w[0]) for w in ff_f32[0][1:4])}

    def ffn(xp, xs):
        n = ff_state["n"]
        norm, _, _, _, layer = ff_f32[n]
        nxt = ff_f32[n + 1][1:] if n + 1 < len(ff_f32) else None
        xp, xs, w_next = _ffn(xp, xs, norm, *ff_state["w"], layer, nxt)
        ff_state["n"], ff_state["w"] = n + 1, w_next
        return xp, xs

    w_attn_in_b, w_attn_out_b = bf(w_attn_in), bf(w_attn_out)
    w_gmlp_in_b, w_gmlp_out_b = bf(w_gmlp_in), bf(w_gmlp_out)

    tabs = (_rope_tables(jnp.arange(S, dtype=jnp.int32)),
            _rope_tables(n_past + jnp.arange(TS, dtype=jnp.int32)))

    new_k_p, new_v_p, new_k_s, new_v_s, gmlp_v_s = [], [], [], [], []
    for i in range(depth):
        xp, xs = ffn(xp, xs)
        hp = _norm(xp, norm_mix, i, 256)
        hs = _norm(xs.reshape(DB * TS, D), norm_mix, i, DB * TS).reshape(DB, TS, D)
        if i % 2 == 0:
            a = i // 2
            lambda_init = 0.8 - 0.6 * math.exp(-0.3 * i)
            lams = (lambda_q1, lambda_k1, lambda_q2, lambda_k2)
            qw = N_HEADS * 2 * HEAD_DIM
            q_p, q_s = _proj(hp, hs, w_attn_in_b, a, 0, qw, epi="norm_rope", gain=q_norm, tabs=tabs,
                             outs=("p_bf16", "s_f32"))
            k_pf, k_pb, k_s = _proj(hp, hs, w_attn_in_b, a, qw, qw, epi="norm_rope", gain=k_norm, tabs=tabs,
                                    outs=("p_f32", "p_bf16", "s_f32"))
            v_pf, v_pb, v_s = _proj(hp, hs, w_attn_in_b, a, 2 * qw, qw, outs=("p_f32", "p_bf16", "s_f32"))
            att_p = _attn_prompt(q_p, k_pb, v_pb, lams, attn_subln, a, B, S, lambda_init)
            o_s = _attn_sample(page_table, _decode_query_matrix(q_s), cache_k, cache_v,
                               k_s[:, :8].reshape(DB, 8, N_HEADS, V_HEAD_DIM),
                               v_s[:, :8].reshape(DB, 8, N_HEADS, V_HEAD_DIM),
                               lams, attn_subln, a, lambda_init)
            o_s = o_s[:, :, :, :HG * 4].reshape(DB, N_HEADS // HG, V_HEAD_DIM, HG, 4)
            o_s = o_s.transpose(0, 4, 1, 3, 2).reshape(DB, 4, D)
            att_s = jnp.pad(o_s, ((0, 0), (0, TS - 4), (0, 0))).astype(BF16)
            xp, xs = _proj(att_p, att_s, w_attn_out_b, a, 0, D, res=(xp, xs), outs=("p_f32", "s_f32"))
            new_k_p.append(k_pf.reshape(B, S, N_HEADS, 2 * HEAD_DIM))
            new_v_p.append(v_pf.reshape(B, S, N_HEADS, V_HEAD_DIM))
            new_k_s.append(k_s[:, :T].reshape(DB, T, N_HEADS, 2 * HEAD_DIM))
            new_v_s.append(v_s[:, :T].reshape(DB, T, N_HEADS, V_HEAD_DIM))
        else:
            g = i // 2
            u_p, u_s = _proj(hp, hs, w_gmlp_in_b, g, 0, D, epi="gelu", outs=("p_bf16", "s_bf16"))
            zv_p, zv_s = _proj(hp, hs, w_gmlp_in_b, g, D, D, epi="gelu", outs=("p_f32", "s_f32"))
            gate_p, gate_s, vn_s = _sgu(zv_p, zv_s, u_p, u_s, gmlp_ln_g, gmlp_ln_b, gmlp_w_s, gmlp_b_s, g)
            xp, xs = _proj(gate_p, gate_s, w_gmlp_out_b, g, 0, D, res=(xp, xs), outs=("p_f32", "s_f32"))
            gmlp_v_s.append(vn_s[:, :T])
        xp, xs = ffn(xp, xs)

    return (xp.reshape(B, S, D), xs[:, :T], jnp.stack(new_k_p), jnp.stack(new_v_p),
            jnp.stack(new_k_s), jnp.stack(new_v_s), jnp.stack(gmlp_v_s))
```

```python
import functools
import math

import jax
import jax.numpy as jnp
from jax import lax
from jax.experimental import pallas as pl
from jax.experimental.pallas import tpu as pltpu

F32 = jnp.float32
BF16 = jnp.bfloat16

D_MODEL = 4096
D_FF = 11008
N_HEADS = 16
HEAD_DIM = 128
V_HEAD_DIM = 256
ROT_DIM = HEAD_DIM // 4
ROPE_THETA = 500000.0
CHUNK = 128
GMLP_GROUPS = 8
GROUP_DIM = D_MODEL // GMLP_GROUPS
EPS = 1e-6

TM = 1024
TS = 16
TR = TM + TS
TF = 256
TN = 512
NORM_ROWS = 208
NEG = -1e30
VMEM_LIMIT = 56 * 1024 * 1024

PAGES_PER_STEP = 4
HG = 8
TQ = 512
TK = 512


def _cparams(n_axes):
    return pltpu.CompilerParams(dimension_semantics=("arbitrary",) * n_axes,
                                vmem_limit_bytes=VMEM_LIMIT)


def _rms(x, g):
    return x * lax.rsqrt(jnp.mean(x * x, axis=-1, keepdims=True) + EPS) * g


def _ffn_kernel(*refs, cast_next):
    if cast_next:
        (xp_hbm, xs_ref, g_ref, wg_ref, wu_ref, wd_ref, ng_ref, nu_ref, nd_ref,
         outp_hbm, outs_ref, cg_ref, cu_ref, cd_ref, acc_ref, h_ref, sem) = refs
        cg_ref[...] = ng_ref[...].astype(BF16)
        cu_ref[...] = nu_ref[...].astype(BF16)
        cd_ref[...] = nd_ref[...].astype(BF16)
    else:
        xp_hbm, xs_ref, g_ref, wg_ref, wu_ref, wd_ref, outp_hbm, outs_ref, acc_ref, h_ref, sem = refs
    i = pl.program_id(0)
    j = pl.program_id(1)
    n_chunks = D_MODEL // TN

    def x_copy():
        return pltpu.make_async_copy(xp_hbm.at[pl.ds(i * TM, TM), :],
                                     acc_ref.at[pl.ds(0, TM), :], sem.at[n_chunks])

    def out_copy(c):
        cols = pl.ds(c * TN, TN)
        return pltpu.make_async_copy(acc_ref.at[pl.ds(0, TM), cols],
                                     outp_hbm.at[pl.ds(i * TM, TM), cols], sem.at[c])

    @pl.when(j == 0)
    def _():
        x_copy().start()
        acc_ref[TM:TR, :] = xs_ref[0]
        x_copy().wait()

        def body(r, c):
            rows = pl.ds(pl.multiple_of(r * NORM_ROWS, NORM_ROWS), NORM_ROWS)
            h_ref[rows, :] = _rms(acc_ref[rows, :], g_ref[...]).astype(BF16)
            return c
        lax.fori_loop(0, TR // NORM_ROWS, body, 0)

    def step(write_back):
        h = h_ref[...]
        gate = jnp.dot(h, wg_ref[...], preferred_element_type=F32)
        up = jnp.dot(h, wu_ref[...], preferred_element_type=F32)
        act = (0.5 * (gate * jax.nn.sigmoid(gate)) * up).astype(BF16)
        for c in range(n_chunks):
            cols = slice(c * TN, (c + 1) * TN)
            acc_ref[:, cols] += jnp.dot(act, wd_ref[:, cols], preferred_element_type=F32)
            if write_back:
                out_copy(c).start()
        if write_back:
            outs_ref[0] = acc_ref[TM:TR, :]
            for c in range(n_chunks):
                out_copy(c).wait()

    last = pl.num_programs(1) - 1
    pl.when(j < last)(functools.partial(step, False))
    pl.when(j == last)(functools.partial(step, True))


def _ffn(xp, xs, g, wg, wu, wd, layer, nxt=None):
    nb = xp.shape[0] // TM
    nj = D_FF // TF
    rows_g = D_MODEL // nb
    in_specs = [
        pl.BlockSpec(memory_space=pl.ANY),
        pl.BlockSpec((1, TS, D_MODEL), lambda i, j: (i, 0, 0)),
        pl.BlockSpec((None, 1, D_MODEL), lambda i, j: (layer, 0, 0)),
        pl.BlockSpec((D_MODEL, TF), lambda i, j: (0, j)),
        pl.BlockSpec((D_MODEL, TF), lambda i, j: (0, j)),
        pl.BlockSpec((TF, D_MODEL), lambda i, j: (j, 0)),
    ]
    out_specs = [
        pl.BlockSpec(memory_space=pl.ANY),
        pl.BlockSpec((1, TS, D_MODEL), lambda i, j: (i, 0, 0)),
    ]
    out_shape = [jax.ShapeDtypeStruct(xp.shape, F32), jax.ShapeDtypeStruct(xs.shape, F32)]
    args = [xp, xs, g.reshape(g.shape[0], 1, D_MODEL), wg, wu, wd]
    if nxt is not None:
        ng, nu, nd, nl = nxt
        in_specs += [pl.BlockSpec((None, rows_g, TF), lambda i, j: (nl, i, j)),
                     pl.BlockSpec((None, rows_g, TF), lambda i, j: (nl, i, j)),
                     pl.BlockSpec((None, TF, rows_g), lambda i, j: (nl, j, i))]
        out_specs += [pl.BlockSpec((rows_g, TF), lambda i, j: (i, j)),
                      pl.BlockSpec((rows_g, TF), lambda i, j: (i, j)),
                      pl.BlockSpec((TF, rows_g), lambda i, j: (j, i))]
        out_shape += [jax.ShapeDtypeStruct((D_MODEL, D_FF), BF16), jax.ShapeDtypeStruct((D_MODEL, D_FF), BF16),
                      jax.ShapeDtypeStruct((D_FF, D_MODEL), BF16)]
        args += [ng, nu, nd]
    res = pl.pallas_call(
        functools.partial(_ffn_kernel, cast_next=nxt is not None),
        grid=(nb, nj),
        in_specs=in_specs,
        out_specs=out_specs,
        out_shape=out_shape,
        scratch_shapes=[pltpu.VMEM((TR, D_MODEL), F32), pltpu.VMEM((TR, D_MODEL), BF16),
                        pltpu.SemaphoreType.DMA((D_MODEL // TN + 1,))],
        compiler_params=_cparams(2),
        name="ffn",
    )(*args)
    return res[0], res[1], tuple(res[2:])


def _norm_kernel(x_ref, g_ref, o_ref):
    o_ref[...] = _rms(x_ref[...], g_ref[...]).astype(BF16)


def _norm(x2d, g, layer, rows):
    return pl.pallas_call(
        _norm_kernel,
        grid=(x2d.shape[0] // rows,),
        in_specs=[pl.BlockSpec((rows, D_MODEL), lambda i: (i, 0)),
                  pl.BlockSpec((None, 1, D_MODEL), lambda i: (layer, 0, 0))],
        out_specs=pl.BlockSpec((rows, D_MODEL), lambda i: (i, 0)),
        out_shape=jax.ShapeDtypeStruct(x2d.shape, BF16),
        compiler_params=_cparams(1),
        name="norm",
    )(x2d, g.reshape(g.shape[0], 1, D_MODEL))


def _norm_rope(y, gain, cos, sin_lo, sin_hi):
    cols = []
    for c in range(y.shape[1] // HEAD_DIM):
        blk = _rms(y[:, c * HEAD_DIM:(c + 1) * HEAD_DIM], gain)
        blk = (blk * cos + pltpu.roll(blk, HEAD_DIM - ROT_DIM // 2, 1) * sin_lo
               + pltpu.roll(blk, ROT_DIM // 2, 1) * sin_hi)
        cols.append(blk)
    return jnp.concatenate(cols, axis=1)


def _gelu(y):
    return 0.5 * y * (1.0 + lax.erf(y * (2.0 ** -0.5)))


def _proj_kernel(*refs, epi, has_res, outs, lagged):
    it = iter(refs)
    lp_ref, ls_ref, w_ref = next(it), next(it), next(it)
    if epi == "norm_rope":
        gain_ref = next(it)
        tab_p = [next(it) for _ in range(3)]
        tab_s = [next(it) for _ in range(3)]
    if has_res:
        rp_ref, rs_ref = next(it), next(it)
    out_refs = {name: next(it) for name in outs}
    lhs_ref = next(it)
    j = pl.program_id(1)

    def load_lhs():
        lhs_ref[0:TM, :] = lp_ref[...]
        lhs_ref[TM:TR, :] = ls_ref[0]

    def matmul():
        return jnp.dot(lhs_ref[...], w_ref[...], preferred_element_type=F32)

    def finish(yp, ys):
        if epi == "norm_rope":
            gain = gain_ref[...]
            yp = _norm_rope(yp, gain, *[t[...] for t in tab_p])
            ys = _norm_rope(ys, gain, *[t[...] for t in tab_s])
        elif epi == "gelu":
            yp, ys = _gelu(yp), _gelu(ys)
        if has_res:
            yp = rp_ref[...] + yp
            ys = rs_ref[0] + ys
        if "p_f32" in out_refs:
            out_refs["p_f32"][...] = yp
        if "p_bf16" in out_refs:
            out_refs["p_bf16"][...] = yp.astype(BF16)
        if "s_f32" in out_refs:
            out_refs["s_f32"][0] = ys
        if "s_bf16" in out_refs:
            out_refs["s_bf16"][0] = ys.astype(BF16)

    if not lagged:
        pl.when(j == 0)(load_lhs)
        y = matmul()
        finish(y[0:TM], y[TM:TR])
        return

    y_refs = (next(it), next(it))
    n_col = pl.num_programs(1) - 1

    @pl.when(j == 0)
    def _():
        load_lhs()
        y_refs[0][...] = matmul()

    for parity in range(2):
        @pl.when((j > 0) & (j < n_col) & (j % 2 == parity))
        def _(parity=parity):
            y_refs[parity][...] = matmul()
            finish(y_refs[1 - parity][0:TM, :], y_refs[1 - parity][TM:TR, :])

    for parity in range(2):
        @pl.when((j == n_col) & ((n_col - 1) % 2 == parity))
        def _(parity=parity):
            finish(y_refs[parity][0:TM, :], y_refs[parity][TM:TR, :])


def _proj(lp, ls, w, layer, col0, n_cols, *, epi="none", gain=None, tabs=None, res=None, outs):
    nb = lp.shape[0] // TM
    k_dim = lp.shape[1]
    cb0 = col0 // TN
    n_col = n_cols // TN
    lagged = epi == "norm_rope"
    assert not (lagged and res is not None)
    w_col = (lambda j: jnp.minimum(j, n_col - 1)) if lagged else (lambda j: j)
    o_col = (lambda j: jnp.maximum(j - 1, 0)) if lagged else (lambda j: j)
    p_spec = pl.BlockSpec((TM, TN), lambda i, j: (i, o_col(j)))
    s_spec = pl.BlockSpec((1, TS, TN), lambda i, j: (i, 0, o_col(j)))
    in_specs = [pl.BlockSpec((TM, k_dim), lambda i, j: (i, 0)),
                pl.BlockSpec((1, TS, k_dim), lambda i, j: (i, 0, 0)),
                pl.BlockSpec((None, k_dim, TN), lambda i, j: (layer, 0, cb0 + w_col(j)))]
    args = [lp, ls, w]
    if epi == "norm_rope":
        pos_blocks = tabs[0][0].shape[0] // TM
        in_specs.append(pl.BlockSpec((None, 1, HEAD_DIM), lambda i, j: (layer, 0, 0)))
        args.append(gain.reshape(gain.shape[0], 1, HEAD_DIM))
        for t in tabs[0]:
            in_specs.append(pl.BlockSpec((TM, HEAD_DIM), lambda i, j: (i % pos_blocks, 0)))
            args.append(t)
        for t in tabs[1]:
            in_specs.append(pl.BlockSpec((TS, HEAD_DIM), lambda i, j: (0, 0)))
            args.append(t)
    if res is not None:
        in_specs += [p_spec, s_spec]
        args += list(res)
    out_specs, out_shape = [], []
    for name in outs:
        dt = F32 if name.endswith("f32") else BF16
        if name.startswith("p_"):
            out_specs.append(p_spec)
            out_shape.append(jax.ShapeDtypeStruct((lp.shape[0], n_cols), dt))
        else:
            out_specs.append(s_spec)
            out_shape.append(jax.ShapeDtypeStruct((nb, TS, n_cols), dt))
    return pl.pallas_call(
        functools.partial(_proj_kernel, epi=epi, has_res=res is not None, outs=tuple(outs), lagged=lagged),
        grid=(nb, n_col + 1 if lagged else n_col),
        in_specs=in_specs,
        out_specs=out_specs,
        out_shape=out_shape,
        scratch_shapes=([pltpu.VMEM((TR, k_dim), BF16)]
                        + [pltpu.VMEM((TR, TN), F32)] * (2 if lagged else 0)),
        compiler_params=_cparams(2),
        name="proj_" + epi + ("_res" if res is not None else ""),
    )(*args)


def _diff_lambda(lq1_ref, lk1_ref, lq2_ref, lk2_ref, lambda_init):
    s1 = jnp.sum(lq1_ref[...] * lk1_ref[...], axis=-1, keepdims=True)
    s2 = jnp.sum(lq2_ref[...] * lk2_ref[...], axis=-1, keepdims=True)
    return jnp.exp(s1) - jnp.exp(s2) + lambda_init


def _nt_dot(a, b):
    return lax.dot_general(a, b, (((1,), (1,)), ((), ())), preferred_element_type=F32)


def _attn_prompt_kernel(q_ref, k_ref, v_ref, lq1, lk1, lq2, lk2, g_ref, o_ref, *, lambda_init, n_q):
    qi = pl.program_id(2)
    scale = HEAD_DIM ** -0.5
    causal = (lax.broadcasted_iota(jnp.int32, (TQ, TK), 1)
              <= lax.broadcasted_iota(jnp.int32, (TQ, TK), 0))

    def attend(n_blocks):
        n_keys = n_blocks * TK
        lam = _diff_lambda(lq1, lk1, lq2, lk2, lambda_init)
        outs = []
        for c in range(2):
            lanes = slice(c * HEAD_DIM, (c + 1) * HEAD_DIM)
            s = _nt_dot(q_ref[:, lanes], k_ref[0:n_keys, lanes]) * scale
            s_diag = jnp.where(causal, s[:, n_keys - TK:], NEG)
            m = jnp.max(s_diag, axis=-1, keepdims=True)
            if n_blocks > 1:
                s_past = s[:, :n_keys - TK]
                m = jnp.maximum(m, jnp.max(s_past, axis=-1, keepdims=True))
                p = jnp.concatenate([jnp.exp(s_past - m), jnp.exp(s_diag - m)], axis=1)
            else:
                p = jnp.exp(s_diag - m)
            l = jnp.sum(p, axis=-1, keepdims=True)
            outs.append(jnp.dot(p.astype(BF16), v_ref[0:n_keys, :], preferred_element_type=F32) / l)
        o = outs[0] - lam * outs[1]
        o_ref[...] = (_rms(o, g_ref[...]) * (1.0 - lambda_init)).astype(BF16)

    for n in range(1, n_q + 1):
        pl.when(qi == n - 1)(functools.partial(attend, n))


def _attn_prompt(q, k, v, lams, subln, layer, batch, seq, lambda_init):
    nq = seq // TQ
    lam_spec = pl.BlockSpec((None, 1, HEAD_DIM), lambda b, h, qi: (layer, 0, 0))
    return pl.pallas_call(
        functools.partial(_attn_prompt_kernel, lambda_init=lambda_init, n_q=nq),
        grid=(batch, N_HEADS, nq),
        in_specs=[pl.BlockSpec((TQ, V_HEAD_DIM), lambda b, h, qi: (b * nq + qi, h)),
                  pl.BlockSpec((seq, V_HEAD_DIM), lambda b, h, qi: (b, h)),
                  pl.BlockSpec((seq, V_HEAD_DIM), lambda b, h, qi: (b, h)),
                  lam_spec, lam_spec, lam_spec, lam_spec,
                  pl.BlockSpec((None, 1, V_HEAD_DIM), lambda b, h, qi: (layer, 0, 0))],
        out_specs=pl.BlockSpec((TQ, V_HEAD_DIM), lambda b, h, qi: (b * nq + qi, h)),
        out_shape=jax.ShapeDtypeStruct(q.shape, BF16),
        compiler_params=_cparams(3),
        name="attn_prompt",
    )(q, k, v, *[x.reshape(x.shape[0], 1, HEAD_DIM) for x in lams],
      subln.reshape(subln.shape[0], 1, V_HEAD_DIM))


def _attn_sample_kernel(pt_ref, wq_ref, *refs, lambda_init):
    del pt_ref
    k_refs = refs[0:PAGES_PER_STEP]
    v_refs = refs[PAGES_PER_STEP:2 * PAGES_PER_STEP]
    (kn_ref, vn_ref, lq1, lk1, lq2, lk2, g_ref, o_ref, m_ref, l_ref, acc_ref) = refs[2 * PAGES_PER_STEP:]
    s_idx = pl.program_id(1)
    scale = HEAD_DIM ** -0.5
    n_groups = N_HEADS // HG

    @pl.when(s_idx == 0)
    def _():
        m_ref[...] = jnp.full(m_ref.shape, NEG, F32)
        l_ref[...] = jnp.zeros(l_ref.shape, F32)
        acc_ref[...] = jnp.zeros(acc_ref.shape, F32)

    sub = lax.broadcasted_iota(jnp.int32, (HG, HEAD_DIM), 0)
    lane = lax.broadcasted_iota(jnp.int32, (HG, HEAD_DIM), 1)
    own = (sub == (lane // 4) % HG) & (lane < 2 * HG * 4)

    def update(kb, vb, hg, valid):
        n_tok = kb.shape[0]
        k2 = kb.reshape(n_tok * HG, V_HEAD_DIM).astype(BF16)
        v2 = vb.reshape(n_tok * HG, V_HEAD_DIM).astype(BF16)
        s = jnp.dot(k2, wq_ref[0, hg], preferred_element_type=F32) * scale
        s = s.reshape(n_tok, HG, HEAD_DIM)
        if valid is not None:
            s = jnp.where(valid, s, NEG)
        m_old = m_ref[hg]
        m_new = jnp.maximum(m_old, jnp.max(s, axis=0))
        alpha = jnp.exp(m_old - m_new)
        p = jnp.exp(s - m_new[None])
        l_ref[hg] = alpha * l_ref[hg] + jnp.sum(p, axis=0)
        m_ref[hg] = m_new
        p_own = jnp.where(own[None], p, 0.0).reshape(n_tok * HG, HEAD_DIM).astype(BF16)
        alpha_row = jnp.sum(jnp.where(own, alpha, 0.0), axis=0, keepdims=True)
        pv = lax.dot_general(v2, p_own, (((0,), (0,)), ((), ())), preferred_element_type=F32)
        acc_ref[hg] = acc_ref[hg] * alpha_row + pv

    for g in range(PAGES_PER_STEP):
        for hg in range(n_groups):
            heads = slice(hg * HG, (hg + 1) * HG)
            update(k_refs[g][:, heads, :], v_refs[g][:, heads, :], hg, None)

    @pl.when(s_idx == pl.num_programs(1) - 1)
    def _():
        n_new = kn_ref.shape[1]
        tok = lax.broadcasted_iota(jnp.int32, (n_new, HG, HEAD_DIM), 0)
        qpos = lax.broadcasted_iota(jnp.int32, (n_new, HG, HEAD_DIM), 2) % 4
        valid = tok <= qpos
        lam = _diff_lambda(lq1, lk1, lq2, lk2, lambda_init)
        for hg in range(n_groups):
            heads = slice(hg * HG, (hg + 1) * HG)
            update(kn_ref[0, :, heads, :], vn_ref[0, :, heads, :], hg, valid)
            l_row = jnp.sum(jnp.where(own, l_ref[hg], 0.0), axis=0, keepdims=True)
            o = acc_ref[hg] / jnp.where(l_row > 0.0, l_row, 1.0)
            o = o - lam * pltpu.roll(o, HEAD_DIM - HG * 4, 1)
            o = o * lax.rsqrt(jnp.mean(o * o, axis=0, keepdims=True) + EPS) * g_ref[...]
            o_ref[0, hg] = o * (1.0 - lambda_init)


def _attn_sample(page_table, wq, cache_k, cache_v, k_new, v_new, lams, subln, layer, lambda_init):
    n_b, n_pages = page_table.shape
    n_steps = n_pages // PAGES_PER_STEP
    n_groups = N_HEADS // HG
    page = cache_k.shape[2]

    def page_spec(g):
        return pl.BlockSpec((None, None, page, N_HEADS, V_HEAD_DIM),
                            lambda b, s, pt: (layer, pt[b * n_pages + s * PAGES_PER_STEP + g], 0, 0, 0))

    lam_spec = pl.BlockSpec((None, 1, HEAD_DIM), lambda b, s, pt: (layer, 0, 0))
    new_spec = pl.BlockSpec((1, 8, N_HEADS, V_HEAD_DIM), lambda b, s, pt: (b, 0, 0, 0))
    grid_spec = pltpu.PrefetchScalarGridSpec(
        num_scalar_prefetch=1,
        grid=(n_b, n_steps),
        in_specs=([pl.BlockSpec((1, n_groups, V_HEAD_DIM, HEAD_DIM), lambda b, s, pt: (b, 0, 0, 0))]
                  + [page_spec(g) for g in range(PAGES_PER_STEP)] * 2
                  + [new_spec, new_spec, lam_spec, lam_spec, lam_spec, lam_spec,
                     pl.BlockSpec((None, V_HEAD_DIM, 1), lambda b, s, pt: (layer, 0, 0))]),
        out_specs=pl.BlockSpec((1, n_groups, V_HEAD_DIM, HEAD_DIM), lambda b, s, pt: (b, 0, 0, 0)),
        scratch_shapes=[pltpu.VMEM((n_groups, HG, HEAD_DIM), F32),
                        pltpu.VMEM((n_groups, HG, HEAD_DIM), F32),
                        pltpu.VMEM((n_groups, V_HEAD_DIM, HEAD_DIM), F32)],
    )
    return pl.pallas_call(
        functools.partial(_attn_sample_kernel, lambda_init=lambda_init),
        grid_spec=grid_spec,
        out_shape=jax.ShapeDtypeStruct((n_b, n_groups, V_HEAD_DIM, HEAD_DIM), F32),
        compiler_params=_cparams(2),
        name="attn_sample",
    )(page_table.reshape(-1), wq, *([cache_k] * PAGES_PER_STEP), *([cache_v] * PAGES_PER_STEP),
      k_new, v_new, *[x.reshape(x.shape[0], 1, HEAD_DIM) for x in lams],
      subln.reshape(subln.shape[0], V_HEAD_DIM, 1))


def _sgu_kernel(zvp_ref, zvs_ref, up_ref, us_ref, lng_ref, lnb_ref, ws_ref, bs_ref,
                gp_ref, gs_ref, vs_out_ref, sum_ref, sq_ref):
    phase = pl.program_id(1)
    grp = pl.program_id(2)
    zp = zvp_ref[...]
    zs = zvs_ref[0]

    @pl.when((phase == 0) & (grp == 0))
    def _():
        sum_ref[...] = jnp.zeros(sum_ref.shape, F32)
        sq_ref[...] = jnp.zeros(sq_ref.shape, F32)

    @pl.when(phase == 0)
    def _():
        sum_ref[0:TM] += jnp.sum(zp, axis=-1, keepdims=True)
        sum_ref[TM:TR] += jnp.sum(zs, axis=-1, keepdims=True)
        sq_ref[0:TM] += jnp.sum(zp * zp, axis=-1, keepdims=True)
        sq_ref[TM:TR] += jnp.sum(zs * zs, axis=-1, keepdims=True)

    @pl.when(phase == 1)
    def _():
        inv_n = 1.0 / D_MODEL
        mu = sum_ref[...] * inv_n
        var = sq_ref[...] * inv_n - mu * mu
        rstd = lax.rsqrt(var + EPS)
        vp = (zp - mu[0:TM]) * rstd[0:TM] * lng_ref[...] + lnb_ref[...]
        vs = (zs - mu[TM:TR]) * rstd[TM:TR] * lng_ref[...] + lnb_ref[...]
        vs_out_ref[0] = vs
        r = lax.broadcasted_iota(jnp.int32, (CHUNK, CHUNK), 0)
        c = lax.broadcasted_iota(jnp.int32, (CHUNK, CHUNK), 1)
        w = jnp.where(c <= r, ws_ref[...], 0.0).astype(BF16)
        bias = bs_ref[...]
        for t in range(TM // CHUNK):
            rows = slice(t * CHUNK, (t + 1) * CHUNK)
            mix = jnp.dot(w, vp[rows].astype(BF16), preferred_element_type=F32) + bias
            gp_ref[rows, :] = (up_ref[rows, :].astype(F32) * mix).astype(BF16)
        vs_chunk = jnp.concatenate([vs, jnp.zeros((CHUNK - TS, GROUP_DIM), F32)], axis=0)
        keep = lax.broadcasted_iota(jnp.int32, (CHUNK, GROUP_DIM), 0) < 4
        vs_chunk = jnp.where(keep, vs_chunk, 0.0).astype(BF16)
        mix_s = jnp.dot(w, vs_chunk, preferred_element_type=F32) + bias
        gs_ref[0] = (us_ref[0].astype(F32) * mix_s[0:TS]).astype(BF16)


def _sgu(zv_p, zv_s, u_p, u_s, ln_g, ln_b, w_s, b_s, layer):
    nb = zv_p.shape[0] // TM
    p_spec = pl.BlockSpec((TM, GROUP_DIM), lambda i, ph, g: (i, g))
    s_spec = pl.BlockSpec((1, TS, GROUP_DIM), lambda i, ph, g: (i, 0, g))
    po_spec = pl.BlockSpec((TM, GROUP_DIM), lambda i, ph, g: (i, g * ph))
    so_spec = pl.BlockSpec((1, TS, GROUP_DIM), lambda i, ph, g: (i, 0, g * ph))
    vec_spec = pl.BlockSpec((None, 1, GROUP_DIM), lambda i, ph, g: (layer, 0, g))
    return pl.pallas_call(
        _sgu_kernel,
        grid=(nb, 2, GMLP_GROUPS),
        in_specs=[p_spec, s_spec, p_spec, s_spec, vec_spec, vec_spec,
                  pl.BlockSpec((None, None, CHUNK, CHUNK), lambda i, ph, g: (layer, g, 0, 0)),
                  pl.BlockSpec((None, None, CHUNK, 1), lambda i, ph, g: (layer, g, 0, 0))],
        out_specs=[po_spec, so_spec, so_spec],
        out_shape=[jax.ShapeDtypeStruct(zv_p.shape, BF16),
                   jax.ShapeDtypeStruct(zv_s.shape, BF16),
                   jax.ShapeDtypeStruct(zv_s.shape, F32)],
        scratch_shapes=[pltpu.VMEM((TR, 1), F32), pltpu.VMEM((TR, 1), F32)],
        compiler_params=_cparams(3),
        name="sgu",
    )(zv_p, zv_s, u_p, u_s, ln_g.reshape(ln_g.shape[0], 1, D_MODEL), ln_b.reshape(ln_b.shape[0], 1, D_MODEL),
      w_s, b_s.reshape(b_s.shape[0], GMLP_GROUPS, CHUNK, 1))


def _rope_tables(pos):
    half = ROT_DIM // 2
    inv = ROPE_THETA ** (-jnp.arange(half, dtype=F32) / half)
    ang = pos.astype(F32)[:, None] * inv[None, :]
    cos, sin = jnp.cos(ang), jnp.sin(ang)
    n = pos.shape[0]
    rest = HEAD_DIM - ROT_DIM
    cos_t = jnp.concatenate([cos, cos, jnp.ones((n, rest), F32)], axis=1)
    sin_lo = jnp.concatenate([-sin, jnp.zeros((n, HEAD_DIM - half), F32)], axis=1)
    sin_hi = jnp.concatenate([jnp.zeros((n, half), F32), sin, jnp.zeros((n, rest), F32)], axis=1)
    return cos_t, sin_lo, sin_hi


def _decode_query_matrix(q_s):
    n_b = q_s.shape[0]
    n_groups = N_HEADS // HG
    q = q_s[:, :4].reshape(n_b, 4, n_groups, HG, 2, HEAD_DIM)
    w = jnp.einsum("bqghcd,ce->bgcdehq", q, jnp.eye(2, dtype=F32))
    w = w.reshape(n_b, n_groups, 2 * HEAD_DIM, 2 * HG * 4)
    w = jnp.pad(w, ((0, 0), (0, 0), (0, 0), (0, HEAD_DIM - 2 * HG * 4)))
    return w.astype(BF16)


def kernel(x_prompt, x_sample, cache_k, cache_v, page_table, norm_ff1, w_ff1_gate, w_ff1_up, w_ff1_down,
           norm_mix, norm_ff2, w_ff2_gate, w_ff2_up, w_ff2_down, w_attn_in, q_norm, k_norm,
           lambda_q1, lambda_k1, lambda_q2, lambda_k2, attn_subln, w_attn_out,
           w_gmlp_in, gmlp_ln_g, gmlp_ln_b, gmlp_w_s, gmlp_b_s, w_gmlp_out):
    B, S, D = x_prompt.shape
    DB, T, _ = x_sample.shape
    depth = norm_ff1.shape[0]
    n_past = page_table.shape[1] * cache_k.shape[2]
    assert D == D_MODEL and (B * S) // TM == DB and S % TM == 0 and T <= 4

    xp = x_prompt.reshape(B * S, D)
    xs = jnp.pad(x_sample, ((0, 0), (0, TS - T), (0, 0)))

    bf = lambda w: w.astype(BF16)
    ff_f32 = []
    for i in range(depth):
        ff_f32.append((norm_ff1, w_ff1_gate, w_ff1_up, w_ff1_down, i))
        ff_f32.append((norm_ff2, w_ff2_gate, w_ff2_up, w_ff2_down, i))
    ff_state = {"n": 0, "w": tuple(bf(w[0]) for w in ff_f32[0][1:4])}

    def ffn(xp, xs):
        n = ff_state["n"]
        norm, _, _, _, layer = ff_f32[n]
        nxt = ff_f32[n + 1][1:] if n + 1 < len(ff_f32) else None
        xp, xs, w_next = _ffn(xp, xs, norm, *ff_state["w"], layer, nxt)
        ff_state["n"], ff_state["w"] = n + 1, w_next
        return xp, xs

    w_attn_in_b, w_attn_out_b = bf(w_attn_in), bf(w_attn_out)
    w_gmlp_in_b, w_gmlp_out_b = bf(w_gmlp_in), bf(w_gmlp_out)

    tabs = (_rope_tables(jnp.arange(S, dtype=jnp.int32)),
            _rope_tables(n_past + jnp.arange(TS, dtype=jnp.int32)))

    new_k_p, new_v_p, new_k_s, new_v_s, gmlp_v_s = [], [], [], [], []
    for i in range(depth):
        xp, xs = ffn(xp, xs)
        hp = _norm(xp, norm_mix, i, 256)
        hs = _norm(xs.reshape(DB * TS, D), norm_mix, i, DB * TS).reshape(DB, TS, D)
        if i % 2 == 0:
            a = i // 2
            lambda_init = 0.8 - 0.6 * math.exp(-0.3 * i)
            lams = (lambda_q1, lambda_k1, lambda_q2, lambda_k2)
            qw = N_HEADS * 2 * HEAD_DIM
            q_p, q_s = _proj(hp, hs, w_attn_in_b, a, 0, qw, epi="norm_rope", gain=q_norm, tabs=tabs,
                             outs=("p_bf16", "s_f32"))
            k_pf, k_pb, k_s = _proj(hp, hs, w_attn_in_b, a, qw, qw, epi="norm_rope", gain=k_norm, tabs=tabs,
                                    outs=("p_f32", "p_bf16", "s_f32"))
            v_pf, v_pb, v_s = _proj(hp, hs, w_attn_in_b, a, 2 * qw, qw, outs=("p_f32", "p_bf16", "s_f32"))
            att_p = _attn_prompt(q_p, k_pb, v_pb, lams, attn_subln, a, B, S, lambda_init)
            o_s = _attn_sample(page_table, _decode_query_matrix(q_s), cache_k, cache_v,
                               k_s[:, :8].reshape(DB, 8, N_HEADS, V_HEAD_DIM),
                               v_s[:, :8].reshape(DB, 8, N_HEADS, V_HEAD_DIM),
                               lams, attn_subln, a, lambda_init)
            o_s = o_s[:, :, :, :HG * 4].reshape(DB, N_HEADS // HG, V_HEAD_DIM, HG, 4)
            o_s = o_s.transpose(0, 4, 1, 3, 2).reshape(DB, 4, D)
            att_s = jnp.pad(o_s, ((0, 0), (0, TS - 4), (0, 0))).astype(BF16)
            xp, xs = _proj(att_p, att_s, w_attn_out_b, a, 0, D, res=(xp, xs), outs=("p_f32", "s_f32"))
            new_k_p.append(k_pf.reshape(B, S, N_HEADS, 2 * HEAD_DIM))
            new_v_p.append(v_pf.reshape(B, S, N_HEADS, V_HEAD_DIM))
            new_k_s.append(k_s[:, :T].reshape(DB, T, N_HEADS, 2 * HEAD_DIM))
            new_v_s.append(v_s[:, :T].reshape(DB, T, N_HEADS, V_HEAD_DIM))
        else:
            g = i // 2
            u_p, u_s = _proj(hp, hs, w_gmlp_in_b, g, 0, D, epi="gelu", outs=("p_bf16", "s_bf16"))
            zv_p, zv_s = _proj(hp, hs, w_gmlp_in_b, g, D, D, epi="gelu", outs=("p_f32", "s_f32"))
            gate_p, gate_s, vn_s = _sgu(zv_p, zv_s, u_p, u_s, gmlp_ln_g, gmlp_ln_b, gmlp_w_s, gmlp_b_s, g)
            xp, xs = _proj(gate_p, gate_s, w_gmlp_out_b, g, 0, D, res=(xp, xs), outs=("p_f32", "s_f32"))
            gmlp_v_s.append(vn_s[:, :T])
        xp, xs = ffn(xp, xs)

    return (xp.reshape(B, S, D), xs[:, :T], jnp.stack(new_k_p), jnp.stack(new_v_p),
            jnp.stack(new_k_s), jnp.stack(new_v_s), jnp.stack(gmlp_v_s))
```

```python
import functools
import math

import jax
import jax.numpy as jnp
from jax import lax
from jax.experimental import pallas as pl
from jax.experimental.pallas import tpu as pltpu

F32 = jnp.float32
BF16 = jnp.bfloat16

D_MODEL = 4096
D_FF = 11008
N_HEADS = 16
HEAD_DIM = 128
V_HEAD_DIM = 256
ROT_DIM = HEAD_DIM // 4
ROPE_THETA = 500000.0
CHUNK = 128
GMLP_GROUPS = 8
GROUP_DIM = D_MODEL // GMLP_GROUPS
EPS = 1e-6

TM = 1024
TS = 16
TR = TM + TS
TF = 256
TN = 512
NORM_ROWS = 208
NEG = -1e30
VMEM_LIMIT = 56 * 1024 * 1024

PAGES_PER_STEP = 4
HG = 8
TQ = 512
TK = 512


def _cparams(n_axes):
    return pltpu.CompilerParams(dimension_semantics=("arbitrary",) * n_axes,
                                vmem_limit_bytes=VMEM_LIMIT)


def _rms(x, g):
    return x * lax.rsqrt(jnp.mean(x * x, axis=-1, keepdims=True) + EPS) * g


def _ffn_kernel(*refs, n_side):
    xp_hbm, xs_ref, g_ref, wg_ref, wu_ref, wd_ref = refs[0:6]
    side_in = refs[6:6 + n_side]
    outp_hbm, outs_ref = refs[6 + n_side:8 + n_side]
    side_out = refs[8 + n_side:8 + 2 * n_side]
    acc_ref, h_ref, sem = refs[8 + 2 * n_side:]
    for src, dst in zip(side_in, side_out):
        dst[...] = src[...].astype(BF16)
    i = pl.program_id(0)
    j = pl.program_id(1)
    n_chunks = D_MODEL // TN

    def x_copy():
        return pltpu.make_async_copy(xp_hbm.at[pl.ds(i * TM, TM), :],
                                     acc_ref.at[pl.ds(0, TM), :], sem.at[n_chunks])

    def out_copy(c):
        cols = pl.ds(c * TN, TN)
        return pltpu.make_async_copy(acc_ref.at[pl.ds(0, TM), cols],
                                     outp_hbm.at[pl.ds(i * TM, TM), cols], sem.at[c])

    @pl.when(j == 0)
    def _():
        x_copy().start()
        acc_ref[TM:TR, :] = xs_ref[0]
        x_copy().wait()

        def body(r, c):
            rows = pl.ds(pl.multiple_of(r * NORM_ROWS, NORM_ROWS), NORM_ROWS)
            h_ref[rows, :] = _rms(acc_ref[rows, :], g_ref[...]).astype(BF16)
            return c
        lax.fori_loop(0, TR // NORM_ROWS, body, 0)

    def step(write_back):
        h = h_ref[...]
        gate = jnp.dot(h, wg_ref[...], preferred_element_type=F32)
        up = jnp.dot(h, wu_ref[...], preferred_element_type=F32)
        act = (0.5 * (gate * jax.nn.sigmoid(gate)) * up).astype(BF16)
        for c in range(n_chunks):
            cols = slice(c * TN, (c + 1) * TN)
            acc_ref[:, cols] += jnp.dot(act, wd_ref[:, cols], preferred_element_type=F32)
            if write_back:
                out_copy(c).start()
        if write_back:
            outs_ref[0] = acc_ref[TM:TR, :]
            for c in range(n_chunks):
                out_copy(c).wait()

    last = pl.num_programs(1) - 1
    pl.when(j < last)(functools.partial(step, False))
    pl.when(j == last)(functools.partial(step, True))


def _ffn(xp, xs, g, wg, wu, wd, layer, nxt=None, extra=()):
    nb = xp.shape[0] // TM
    nj = D_FF // TF
    rows_g = D_MODEL // nb
    in_specs = [
        pl.BlockSpec(memory_space=pl.ANY),
        pl.BlockSpec((1, TS, D_MODEL), lambda i, j: (i, 0, 0)),
        pl.BlockSpec((None, 1, D_MODEL), lambda i, j: (layer, 0, 0)),
        pl.BlockSpec((D_MODEL, TF), lambda i, j: (0, j)),
        pl.BlockSpec((D_MODEL, TF), lambda i, j: (0, j)),
        pl.BlockSpec((TF, D_MODEL), lambda i, j: (j, 0)),
    ]
    out_specs = [
        pl.BlockSpec(memory_space=pl.ANY),
        pl.BlockSpec((1, TS, D_MODEL), lambda i, j: (i, 0, 0)),
    ]
    out_shape = [jax.ShapeDtypeStruct(xp.shape, F32), jax.ShapeDtypeStruct(xs.shape, F32)]
    args = [xp, xs, g.reshape(g.shape[0], 1, D_MODEL), wg, wu, wd]
    if nxt is not None:
        ng, nu, nd, nl = nxt
        in_specs += [pl.BlockSpec((None, rows_g, TF), lambda i, j: (nl, i, j)),
                     pl.BlockSpec((None, rows_g, TF), lambda i, j: (nl, i, j)),
                     pl.BlockSpec((None, TF, rows_g), lambda i, j: (nl, j, i))]
        out_specs += [pl.BlockSpec((rows_g, TF), lambda i, j: (i, j)),
                      pl.BlockSpec((rows_g, TF), lambda i, j: (i, j)),
                      pl.BlockSpec((TF, rows_g), lambda i, j: (j, i))]
        out_shape += [jax.ShapeDtypeStruct((D_MODEL, D_FF), BF16), jax.ShapeDtypeStruct((D_MODEL, D_FF), BF16),
                      jax.ShapeDtypeStruct((D_FF, D_MODEL), BF16)]
        args += [ng, nu, nd]
    for w, wl, ct in extra:
        n_ct = w.shape[2] // ct
        assert w.shape[1] == D_MODEL and n_ct <= nj
        in_specs.append(pl.BlockSpec((None, rows_g, ct),
                                     lambda i, j, wl=wl, n_ct=n_ct: (wl, i, jnp.minimum(j, n_ct - 1))))
        out_specs.append(pl.BlockSpec((rows_g, ct), lambda i, j, n_ct=n_ct: (i, jnp.minimum(j, n_ct - 1))))
        out_shape.append(jax.ShapeDtypeStruct(w.shape[1:], BF16))
        args.append(w)
    res = pl.pallas_call(
        functools.partial(_ffn_kernel, n_side=len(args) - 6),
        grid=(nb, nj),
        in_specs=in_specs,
        out_specs=out_specs,
        out_shape=out_shape,
        scratch_shapes=[pltpu.VMEM((TR, D_MODEL), F32), pltpu.VMEM((TR, D_MODEL), BF16),
                        pltpu.SemaphoreType.DMA((D_MODEL // TN + 1,))],
        compiler_params=_cparams(2),
        name="ffn",
    )(*args)
    return res[0], res[1], tuple(res[2:])


def _norm_kernel(x_ref, g_ref, o_ref):
    o_ref[...] = _rms(x_ref[...], g_ref[...]).astype(BF16)


def _norm(x2d, g, layer, rows):
    return pl.pallas_call(
        _norm_kernel,
        grid=(x2d.shape[0] // rows,),
        in_specs=[pl.BlockSpec((rows, D_MODEL), lambda i: (i, 0)),
                  pl.BlockSpec((None, 1, D_MODEL), lambda i: (layer, 0, 0))],
        out_specs=pl.BlockSpec((rows, D_MODEL), lambda i: (i, 0)),
        out_shape=jax.ShapeDtypeStruct(x2d.shape, BF16),
        compiler_params=_cparams(1),
        name="norm",
    )(x2d, g.reshape(g.shape[0], 1, D_MODEL))


def _norm_rope(y, gain, cos, sin_lo, sin_hi):
    cols = []
    for c in range(y.shape[1] // HEAD_DIM):
        blk = _rms(y[:, c * HEAD_DIM:(c + 1) * HEAD_DIM], gain)
        blk = (blk * cos + pltpu.roll(blk, HEAD_DIM - ROT_DIM // 2, 1) * sin_lo
               + pltpu.roll(blk, ROT_DIM // 2, 1) * sin_hi)
        cols.append(blk)
    return jnp.concatenate(cols, axis=1)


def _gelu(y):
    return 0.5 * y * (1.0 + lax.erf(y * (2.0 ** -0.5)))


def _proj_kernel(*refs, epi, has_res, outs, lagged):
    it = iter(refs)
    lp_ref, ls_ref, w_ref = next(it), next(it), next(it)
    if epi == "norm_rope":
        gain_ref = next(it)
        tab_p = [next(it) for _ in range(3)]
        tab_s = [next(it) for _ in range(3)]
    if has_res:
        rp_ref, rs_ref = next(it), next(it)
    out_refs = {name: next(it) for name in outs}
    lhs_ref = next(it)
    j = pl.program_id(1)

    def load_lhs():
        lhs_ref[0:TM, :] = lp_ref[...]
        lhs_ref[TM:TR, :] = ls_ref[0]

    def matmul():
        return jnp.dot(lhs_ref[...], w_ref[...], preferred_element_type=F32)

    def finish(yp, ys):
        if epi == "norm_rope":
            gain = gain_ref[...]
            yp = _norm_rope(yp, gain, *[t[...] for t in tab_p])
            ys = _norm_rope(ys, gain, *[t[...] for t in tab_s])
        elif epi == "gelu":
            yp, ys = _gelu(yp), _gelu(ys)
        if has_res:
            yp = rp_ref[...] + yp
            ys = rs_ref[0] + ys
        if "p_f32" in out_refs:
            out_refs["p_f32"][...] = yp
        if "p_bf16" in out_refs:
            out_refs["p_bf16"][...] = yp.astype(BF16)
        if "s_f32" in out_refs:
            out_refs["s_f32"][0] = ys
        if "s_bf16" in out_refs:
            out_refs["s_bf16"][0] = ys.astype(BF16)

    if not lagged:
        pl.when(j == 0)(load_lhs)
        y = matmul()
        finish(y[0:TM], y[TM:TR])
        return

    y_refs = (next(it), next(it))
    n_col = pl.num_programs(1) - 1

    @pl.when(j == 0)
    def _():
        load_lhs()
        y_refs[0][...] = matmul()

    for parity in range(2):
        @pl.when((j > 0) & (j < n_col) & (j % 2 == parity))
        def _(parity=parity):
            y_refs[parity][...] = matmul()
            finish(y_refs[1 - parity][0:TM, :], y_refs[1 - parity][TM:TR, :])

    for parity in range(2):
        @pl.when((j == n_col) & ((n_col - 1) % 2 == parity))
        def _(parity=parity):
            finish(y_refs[parity][0:TM, :], y_refs[parity][TM:TR, :])


def _proj(lp, ls, w, layer, col0, n_cols, *, epi="none", gain=None, tabs=None, res=None, outs):
    nb = lp.shape[0] // TM
    k_dim = lp.shape[1]
    cb0 = col0 // TN
    n_col = n_cols // TN
    lagged = epi == "norm_rope"
    assert not (lagged and res is not None)
    w_col = (lambda j: jnp.minimum(j, n_col - 1)) if lagged else (lambda j: j)
    o_col = (lambda j: jnp.maximum(j - 1, 0)) if lagged else (lambda j: j)
    p_spec = pl.BlockSpec((TM, TN), lambda i, j: (i, o_col(j)))
    s_spec = pl.BlockSpec((1, TS, TN), lambda i, j: (i, 0, o_col(j)))
    in_specs = [pl.BlockSpec((TM, k_dim), lambda i, j: (i, 0)),
                pl.BlockSpec((1, TS, k_dim), lambda i, j: (i, 0, 0)),
                pl.BlockSpec((None, k_dim, TN), lambda i, j: (0, 0, cb0 + w_col(j)))]
    args = [lp, ls, w]
    if epi == "norm_rope":
        pos_blocks = tabs[0][0].shape[0] // TM
        in_specs.append(pl.BlockSpec((None, 1, HEAD_DIM), lambda i, j: (layer, 0, 0)))
        args.append(gain.reshape(gain.shape[0], 1, HEAD_DIM))
        for t in tabs[0]:
            in_specs.append(pl.BlockSpec((TM, HEAD_DIM), lambda i, j: (i % pos_blocks, 0)))
            args.append(t)
        for t in tabs[1]:
            in_specs.append(pl.BlockSpec((TS, HEAD_DIM), lambda i, j: (0, 0)))
            args.append(t)
    if res is not None:
        in_specs += [p_spec, s_spec]
        args += list(res)
    out_specs, out_shape = [], []
    for name in outs:
        dt = F32 if name.endswith("f32") else BF16
        if name.startswith("p_"):
            out_specs.append(p_spec)
            out_shape.append(jax.ShapeDtypeStruct((lp.shape[0], n_cols), dt))
        else:
            out_specs.append(s_spec)
            out_shape.append(jax.ShapeDtypeStruct((nb, TS, n_cols), dt))
    return pl.pallas_call(
        functools.partial(_proj_kernel, epi=epi, has_res=res is not None, outs=tuple(outs), lagged=lagged),
        grid=(nb, n_col + 1 if lagged else n_col),
        in_specs=in_specs,
        out_specs=out_specs,
        out_shape=out_shape,
        scratch_shapes=([pltpu.VMEM((TR, k_dim), BF16)]
                        + [pltpu.VMEM((TR, TN), F32)] * (2 if lagged else 0)),
        compiler_params=_cparams(2),
        name="proj_" + epi + ("_res" if res is not None else ""),
    )(*args)


def _diff_lambda(lq1_ref, lk1_ref, lq2_ref, lk2_ref, lambda_init):
    s1 = jnp.sum(lq1_ref[...] * lk1_ref[...], axis=-1, keepdims=True)
    s2 = jnp.sum(lq2_ref[...] * lk2_ref[...], axis=-1, keepdims=True)
    return jnp.exp(s1) - jnp.exp(s2) + lambda_init


def _nt_dot(a, b):
    return lax.dot_general(a, b, (((1,), (1,)), ((), ())), preferred_element_type=F32)


def _attn_prompt_kernel(q_ref, k_ref, v_ref, lq1, lk1, lq2, lk2, g_ref, o_ref, *, lambda_init, n_q):
    qi = pl.program_id(2)
    scale = HEAD_DIM ** -0.5
    causal = (lax.broadcasted_iota(jnp.int32, (TQ, TK), 1)
              <= lax.broadcasted_iota(jnp.int32, (TQ, TK), 0))

    def attend(n_blocks):
        n_keys = n_blocks * TK
        lam = _diff_lambda(lq1, lk1, lq2, lk2, lambda_init)
        outs = []
        for c in range(2):
            lanes = slice(c * HEAD_DIM, (c + 1) * HEAD_DIM)
            s = _nt_dot(q_ref[:, lanes], k_ref[0:n_keys, lanes]) * scale
            s_diag = jnp.where(causal, s[:, n_keys - TK:], NEG)
            m = jnp.max(s_diag, axis=-1, keepdims=True)
            if n_blocks > 1:
                s_past = s[:, :n_keys - TK]
                m = jnp.maximum(m, jnp.max(s_past, axis=-1, keepdims=True))
                p = jnp.concatenate([jnp.exp(s_past - m), jnp.exp(s_diag - m)], axis=1)
            else:
                p = jnp.exp(s_diag - m)
            l = jnp.sum(p, axis=-1, keepdims=True)
            outs.append(jnp.dot(p.astype(BF16), v_ref[0:n_keys, :], preferred_element_type=F32) / l)
        o = outs[0] - lam * outs[1]
        o_ref[...] = (_rms(o, g_ref[...]) * (1.0 - lambda_init)).astype(BF16)

    for n in range(1, n_q + 1):
        pl.when(qi == n - 1)(functools.partial(attend, n))


def _attn_prompt(q, k, v, lams, subln, layer, batch, seq, lambda_init):
    nq = seq // TQ
    lam_spec = pl.BlockSpec((None, 1, HEAD_DIM), lambda b, h, qi: (layer, 0, 0))
    return pl.pallas_call(
        functools.partial(_attn_prompt_kernel, lambda_init=lambda_init, n_q=nq),
        grid=(batch, N_HEADS, nq),
        in_specs=[pl.BlockSpec((TQ, V_HEAD_DIM), lambda b, h, qi: (b * nq + qi, h)),
                  pl.BlockSpec((seq, V_HEAD_DIM), lambda b, h, qi: (b, h)),
                  pl.BlockSpec((seq, V_HEAD_DIM), lambda b, h, qi: (b, h)),
                  lam_spec, lam_spec, lam_spec, lam_spec,
                  pl.BlockSpec((None, 1, V_HEAD_DIM), lambda b, h, qi: (layer, 0, 0))],
        out_specs=pl.BlockSpec((TQ, V_HEAD_DIM), lambda b, h, qi: (b * nq + qi, h)),
        out_shape=jax.ShapeDtypeStruct(q.shape, BF16),
        compiler_params=_cparams(3),
        name="attn_prompt",
    )(q, k, v, *[x.reshape(x.shape[0], 1, HEAD_DIM) for x in lams],
      subln.reshape(subln.shape[0], 1, V_HEAD_DIM))


def _attn_sample_kernel(pt_ref, wq_ref, *refs, lambda_init):
    del pt_ref
    k_refs = refs[0:PAGES_PER_STEP]
    v_refs = refs[PAGES_PER_STEP:2 * PAGES_PER_STEP]
    (kn_ref, vn_ref, lq1, lk1, lq2, lk2, g_ref, o_ref, m_ref, l_ref, acc_ref) = refs[2 * PAGES_PER_STEP:]
    s_idx = pl.program_id(1)
    scale = HEAD_DIM ** -0.5
    n_groups = N_HEADS // HG

    @pl.when(s_idx == 0)
    def _():
        m_ref[...] = jnp.full(m_ref.shape, NEG, F32)
        l_ref[...] = jnp.zeros(l_ref.shape, F32)
        acc_ref[...] = jnp.zeros(acc_ref.shape, F32)

    sub = lax.broadcasted_iota(jnp.int32, (HG, HEAD_DIM), 0)
    lane = lax.broadcasted_iota(jnp.int32, (HG, HEAD_DIM), 1)
    own = (sub == (lane // 4) % HG) & (lane < 2 * HG * 4)

    def update(kb, vb, hg, valid):
        n_tok = kb.shape[0]
        k2 = kb.reshape(n_tok * HG, V_HEAD_DIM).astype(BF16)
        v2 = vb.reshape(n_tok * HG, V_HEAD_DIM).astype(BF16)
        s = jnp.dot(k2, wq_ref[0, hg], preferred_element_type=F32) * scale
        s = s.reshape(n_tok, HG, HEAD_DIM)
        if valid is not None:
            s = jnp.where(valid, s, NEG)
        m_old = m_ref[hg]
        m_new = jnp.maximum(m_old, jnp.max(s, axis=0))
        alpha = jnp.exp(m_old - m_new)
        p = jnp.exp(s - m_new[None])
        l_ref[hg] = alpha * l_ref[hg] + jnp.sum(p, axis=0)
        m_ref[hg] = m_new
        p_own = jnp.where(own[None], p, 0.0).reshape(n_tok * HG, HEAD_DIM).astype(BF16)
        alpha_row = jnp.sum(jnp.where(own, alpha, 0.0), axis=0, keepdims=True)
        pv = lax.dot_general(v2, p_own, (((0,), (0,)), ((), ())), preferred_element_type=F32)
        acc_ref[hg] = acc_ref[hg] * alpha_row + pv

    for g in range(PAGES_PER_STEP):
        for hg in range(n_groups):
            heads = slice(hg * HG, (hg + 1) * HG)
            update(k_refs[g][:, heads, :], v_refs[g][:, heads, :], hg, None)

    @pl.when(s_idx == pl.num_programs(1) - 1)
    def _():
        n_new = kn_ref.shape[1]
        tok = lax.broadcasted_iota(jnp.int32, (n_new, HG, HEAD_DIM), 0)
        qpos = lax.broadcasted_iota(jnp.int32, (n_new, HG, HEAD_DIM), 2) % 4
        valid = tok <= qpos
        lam = _diff_lambda(lq1, lk1, lq2, lk2, lambda_init)
        for hg in range(n_groups):
            heads = slice(hg * HG, (hg + 1) * HG)
            update(kn_ref[0, :, heads, :], vn_ref[0, :, heads, :], hg, valid)
            l_row = jnp.sum(jnp.where(own, l_ref[hg], 0.0), axis=0, keepdims=True)
            o = acc_ref[hg] / jnp.where(l_row > 0.0, l_row, 1.0)
            o = o - lam * pltpu.roll(o, HEAD_DIM - HG * 4, 1)
            o = o * lax.rsqrt(jnp.mean(o * o, axis=0, keepdims=True) + EPS) * g_ref[...]
            o_ref[0, hg] = o * (1.0 - lambda_init)


def _attn_sample(page_table, wq, cache_k, cache_v, k_new, v_new, lams, subln, layer, lambda_init):
    n_b, n_pages = page_table.shape
    n_steps = n_pages // PAGES_PER_STEP
    n_groups = N_HEADS // HG
    page = cache_k.shape[2]

    def page_spec(g):
        return pl.BlockSpec((None, None, page, N_HEADS, V_HEAD_DIM),
                            lambda b, s, pt: (layer, pt[b * n_pages + s * PAGES_PER_STEP + g], 0, 0, 0))

    lam_spec = pl.BlockSpec((None, 1, HEAD_DIM), lambda b, s, pt: (layer, 0, 0))
    new_spec = pl.BlockSpec((1, 8, N_HEADS, V_HEAD_DIM), lambda b, s, pt: (b, 0, 0, 0))
    grid_spec = pltpu.PrefetchScalarGridSpec(
        num_scalar_prefetch=1,
        grid=(n_b, n_steps),
        in_specs=([pl.BlockSpec((1, n_groups, V_HEAD_DIM, HEAD_DIM), lambda b, s, pt: (b, 0, 0, 0))]
                  + [page_spec(g) for g in range(PAGES_PER_STEP)] * 2
                  + [new_spec, new_spec, lam_spec, lam_spec, lam_spec, lam_spec,
                     pl.BlockSpec((None, V_HEAD_DIM, 1), lambda b, s, pt: (layer, 0, 0))]),
        out_specs=pl.BlockSpec((1, n_groups, V_HEAD_DIM, HEAD_DIM), lambda b, s, pt: (b, 0, 0, 0)),
        scratch_shapes=[pltpu.VMEM((n_groups, HG, HEAD_DIM), F32),
                        pltpu.VMEM((n_groups, HG, HEAD_DIM), F32),
                        pltpu.VMEM((n_groups, V_HEAD_DIM, HEAD_DIM), F32)],
    )
    return pl.pallas_call(
        functools.partial(_attn_sample_kernel, lambda_init=lambda_init),
        grid_spec=grid_spec,
        out_shape=jax.ShapeDtypeStruct((n_b, n_groups, V_HEAD_DIM, HEAD_DIM), F32),
        compiler_params=_cparams(2),
        name="attn_sample",
    )(page_table.reshape(-1), wq, *([cache_k] * PAGES_PER_STEP), *([cache_v] * PAGES_PER_STEP),
      k_new, v_new, *[x.reshape(x.shape[0], 1, HEAD_DIM) for x in lams],
      subln.reshape(subln.shape[0], V_HEAD_DIM, 1))


def _sgu_kernel(zvp_ref, zvs_ref, up_ref, us_ref, lng_ref, lnb_ref, ws_ref, bs_ref,
                gp_ref, gs_ref, vs_out_ref, sum_ref, sq_ref):
    phase = pl.program_id(1)
    grp = pl.program_id(2)
    zp = zvp_ref[...]
    zs = zvs_ref[0]

    @pl.when((phase == 0) & (grp == 0))
    def _():
        sum_ref[...] = jnp.zeros(sum_ref.shape, F32)
        sq_ref[...] = jnp.zeros(sq_ref.shape, F32)

    @pl.when(phase == 0)
    def _():
        sum_ref[0:TM] += jnp.sum(zp, axis=-1, keepdims=True)
        sum_ref[TM:TR] += jnp.sum(zs, axis=-1, keepdims=True)
        sq_ref[0:TM] += jnp.sum(zp * zp, axis=-1, keepdims=True)
        sq_ref[TM:TR] += jnp.sum(zs * zs, axis=-1, keepdims=True)

    @pl.when(phase == 1)
    def _():
        inv_n = 1.0 / D_MODEL
        mu = sum_ref[...] * inv_n
        var = sq_ref[...] * inv_n - mu * mu
        rstd = lax.rsqrt(var + EPS)
        vp = (zp - mu[0:TM]) * rstd[0:TM] * lng_ref[...] + lnb_ref[...]
        vs = (zs - mu[TM:TR]) * rstd[TM:TR] * lng_ref[...] + lnb_ref[...]
        vs_out_ref[0] = vs
        r = lax.broadcasted_iota(jnp.int32, (CHUNK, CHUNK), 0)
        c = lax.broadcasted_iota(jnp.int32, (CHUNK, CHUNK), 1)
        w = jnp.where(c <= r, ws_ref[...], 0.0).astype(BF16)
        bias = bs_ref[...]
        for t in range(TM // CHUNK):
            rows = slice(t * CHUNK, (t + 1) * CHUNK)
            mix = jnp.dot(w, vp[rows].astype(BF16), preferred_element_type=F32) + bias
            gp_ref[rows, :] = (up_ref[rows, :].astype(F32) * mix).astype(BF16)
        vs_chunk = jnp.concatenate([vs, jnp.zeros((CHUNK - TS, GROUP_DIM), F32)], axis=0)
        keep = lax.broadcasted_iota(jnp.int32, (CHUNK, GROUP_DIM), 0) < 4
        vs_chunk = jnp.where(keep, vs_chunk, 0.0).astype(BF16)
        mix_s = jnp.dot(w, vs_chunk, preferred_element_type=F32) + bias
        gs_ref[0] = (us_ref[0].astype(F32) * mix_s[0:TS]).astype(BF16)


def _sgu(zv_p, zv_s, u_p, u_s, ln_g, ln_b, w_s, b_s, layer):
    nb = zv_p.shape[0] // TM
    p_spec = pl.BlockSpec((TM, GROUP_DIM), lambda i, ph, g: (i, g))
    s_spec = pl.BlockSpec((1, TS, GROUP_DIM), lambda i, ph, g: (i, 0, g))
    po_spec = pl.BlockSpec((TM, GROUP_DIM), lambda i, ph, g: (i, g * ph))
    so_spec = pl.BlockSpec((1, TS, GROUP_DIM), lambda i, ph, g: (i, 0, g * ph))
    vec_spec = pl.BlockSpec((None, 1, GROUP_DIM), lambda i, ph, g: (layer, 0, g))
    return pl.pallas_call(
        _sgu_kernel,
        grid=(nb, 2, GMLP_GROUPS),
        in_specs=[p_spec, s_spec, p_spec, s_spec, vec_spec, vec_spec,
                  pl.BlockSpec((None, None, CHUNK, CHUNK), lambda i, ph, g: (layer, g, 0, 0)),
                  pl.BlockSpec((None, None, CHUNK, 1), lambda i, ph, g: (layer, g, 0, 0))],
        out_specs=[po_spec, so_spec, so_spec],
        out_shape=[jax.ShapeDtypeStruct(zv_p.shape, BF16),
                   jax.ShapeDtypeStruct(zv_s.shape, BF16),
                   jax.ShapeDtypeStruct(zv_s.shape, F32)],
        scratch_shapes=[pltpu.VMEM((TR, 1), F32), pltpu.VMEM((TR, 1), F32)],
        compiler_params=_cparams(3),
        name="sgu",
    )(zv_p, zv_s, u_p, u_s, ln_g.reshape(ln_g.shape[0], 1, D_MODEL), ln_b.reshape(ln_b.shape[0], 1, D_MODEL),
      w_s, b_s.reshape(b_s.shape[0], GMLP_GROUPS, CHUNK, 1))


def _rope_tables(pos):
    half = ROT_DIM // 2
    inv = ROPE_THETA ** (-jnp.arange(half, dtype=F32) / half)
    ang = pos.astype(F32)[:, None] * inv[None, :]
    cos, sin = jnp.cos(ang), jnp.sin(ang)
    n = pos.shape[0]
    rest = HEAD_DIM - ROT_DIM
    cos_t = jnp.concatenate([cos, cos, jnp.ones((n, rest), F32)], axis=1)
    sin_lo = jnp.concatenate([-sin, jnp.zeros((n, HEAD_DIM - half), F32)], axis=1)
    sin_hi = jnp.concatenate([jnp.zeros((n, half), F32), sin, jnp.zeros((n, rest), F32)], axis=1)
    return cos_t, sin_lo, sin_hi


def _decode_query_matrix(q_s):
    n_b = q_s.shape[0]
    n_groups = N_HEADS // HG
    q = q_s[:, :4].reshape(n_b, 4, n_groups, HG, 2, HEAD_DIM)
    w = jnp.einsum("bqghcd,ce->bgcdehq", q, jnp.eye(2, dtype=F32))
    w = w.reshape(n_b, n_groups, 2 * HEAD_DIM, 2 * HG * 4)
    w = jnp.pad(w, ((0, 0), (0, 0), (0, 0), (0, HEAD_DIM - 2 * HG * 4)))
    return w.astype(BF16)


def kernel(x_prompt, x_sample, cache_k, cache_v, page_table, norm_ff1, w_ff1_gate, w_ff1_up, w_ff1_down,
           norm_mix, norm_ff2, w_ff2_gate, w_ff2_up, w_ff2_down, w_attn_in, q_norm, k_norm,
           lambda_q1, lambda_k1, lambda_q2, lambda_k2, attn_subln, w_attn_out,
           w_gmlp_in, gmlp_ln_g, gmlp_ln_b, gmlp_w_s, gmlp_b_s, w_gmlp_out):
    B, S, D = x_prompt.shape
    DB, T, _ = x_sample.shape
    depth = norm_ff1.shape[0]
    n_past = page_table.shape[1] * cache_k.shape[2]
    assert D == D_MODEL and (B * S) // TM == DB and S % TM == 0 and T <= 4

    xp = x_prompt.reshape(B * S, D)
    xs = jnp.pad(x_sample, ((0, 0), (0, TS - T), (0, 0)))

    bf = lambda w: w.astype(BF16)
    ff_f32 = []
    for i in range(depth):
        ff_f32.append((norm_ff1, w_ff1_gate, w_ff1_up, w_ff1_down, i))
        ff_f32.append((norm_ff2, w_ff2_gate, w_ff2_up, w_ff2_down, i))
    ff_state = {"n": 0, "w": tuple(bf(w[0]) for w in ff_f32[0][1:4])}

    def ffn(xp, xs, extra=()):
        n = ff_state["n"]
        norm, _, _, _, layer = ff_f32[n]
        nxt = ff_f32[n + 1][1:] if n + 1 < len(ff_f32) else None
        xp, xs, cast = _ffn(xp, xs, norm, *ff_state["w"], layer, nxt, extra)
        n_next = 3 if nxt is not None else 0
        ff_state["n"], ff_state["w"] = n + 1, cast[:n_next]
        return xp, xs, [w[None] for w in cast[n_next:]]

    tabs = (_rope_tables(jnp.arange(S, dtype=jnp.int32)),
            _rope_tables(n_past + jnp.arange(TS, dtype=jnp.int32)))

    new_k_p, new_v_p, new_k_s, new_v_s, gmlp_v_s = [], [], [], [], []
    for i in range(depth):
        if i % 2 == 0:
            mixer_f32 = ((w_attn_in, i // 2, 384), (w_attn_out, i // 2, 128))
        else:
            mixer_f32 = ((w_gmlp_in, i // 2, 256), (w_gmlp_out, i // 2, 128))
        xp, xs, (w_mix_in, w_mix_out) = ffn(xp, xs, mixer_f32)
        hp = _norm(xp, norm_mix, i, 256)
        hs = _norm(xs.reshape(DB * TS, D), norm_mix, i, DB * TS).reshape(DB, TS, D)
        if i % 2 == 0:
            a = i // 2
            lambda_init = 0.8 - 0.6 * math.exp(-0.3 * i)
            lams = (lambda_q1, lambda_k1, lambda_q2, lambda_k2)
            qw = N_HEADS * 2 * HEAD_DIM
            q_p, q_s = _proj(hp, hs, w_mix_in, a, 0, qw, epi="norm_rope", gain=q_norm, tabs=tabs,
                             outs=("p_bf16", "s_f32"))
            k_pf, k_pb, k_s = _proj(hp, hs, w_mix_in, a, qw, qw, epi="norm_rope", gain=k_norm, tabs=tabs,
                                    outs=("p_f32", "p_bf16", "s_f32"))
            v_pf, v_pb, v_s = _proj(hp, hs, w_mix_in, a, 2 * qw, qw, outs=("p_f32", "p_bf16", "s_f32"))
            att_p = _attn_prompt(q_p, k_pb, v_pb, lams, attn_subln, a, B, S, lambda_init)
            o_s = _attn_sample(page_table, _decode_query_matrix(q_s), cache_k, cache_v,
                               k_s[:, :8].reshape(DB, 8, N_HEADS, V_HEAD_DIM),
                               v_s[:, :8].reshape(DB, 8, N_HEADS, V_HEAD_DIM),
                               lams, attn_subln, a, lambda_init)
            o_s = o_s[:, :, :, :HG * 4].reshape(DB, N_HEADS // HG, V_HEAD_DIM, HG, 4)
            o_s = o_s.transpose(0, 4, 1, 3, 2).reshape(DB, 4, D)
            att_s = jnp.pad(o_s, ((0, 0), (0, TS - 4), (0, 0))).astype(BF16)
            xp, xs = _proj(att_p, att_s, w_mix_out, a, 0, D, res=(xp, xs), outs=("p_f32", "s_f32"))
            new_k_p.append(k_pf.reshape(B, S, N_HEADS, 2 * HEAD_DIM))
            new_v_p.append(v_pf.reshape(B, S, N_HEADS, V_HEAD_DIM))
            new_k_s.append(k_s[:, :T].reshape(DB, T, N_HEADS, 2 * HEAD_DIM))
            new_v_s.append(v_s[:, :T].reshape(DB, T, N_HEADS, V_HEAD_DIM))
        else:
            g = i // 2
            u_p, u_s = _proj(hp, hs, w_mix_in, g, 0, D, epi="gelu", outs=("p_bf16", "s_bf16"))
            zv_p, zv_s = _proj(hp, hs, w_mix_in, g, D, D, epi="gelu", outs=("p_f32", "s_f32"))
            gate_p, gate_s, vn_s = _sgu(zv_p, zv_s, u_p, u_s, gmlp_ln_g, gmlp_ln_b, gmlp_w_s, gmlp_b_s, g)
            xp, xs = _proj(gate_p, gate_s, w_mix_out, g, 0, D, res=(xp, xs), outs=("p_f32", "s_f32"))
            gmlp_v_s.append(vn_s[:, :T])
        xp, xs, _ = ffn(xp, xs)

    return (xp.reshape(B, S, D), xs[:, :T], jnp.stack(new_k_p), jnp.stack(new_v_p),
            jnp.stack(new_k_s), jnp.stack(new_v_s), jnp.stack(gmlp_v_s))
```
